```python
import jax, jax.numpy as jnp
from jax import lax
import numpy as np

D_MODEL = 4096
BATCH = 1
SEQ = 16384
DEPTH = 4

N_MIXERS = 2
N_A_LAYERS = (DEPTH + 1) // 2
N_B_LAYERS = DEPTH // 2
CHUNK = 128
SGU_WIDTH = D_MODEL
N_GROUPS = 32
GROUP_DIM = SGU_WIDTH // N_GROUPS
N_HEADS = 16
HEAD_DIM = D_MODEL // N_HEADS
Q_BLOCK = 128
D_FF = -(-(8 * D_MODEL) // (3 * 256)) * 256
NORM_EPS = 1e-6
LN_EPS = 1e-5

kernel_name = "hybrid_sgu_stickbreaking_trunk"


def rmsnorm(x, g):
    xf = x.astype(jnp.float32)
    y = xf * lax.rsqrt(jnp.mean(xf * xf, axis=-1, keepdims=True) + NORM_EPS)
    return (y * g.astype(jnp.float32)).astype(x.dtype)


def layernorm(x, g, b):
    xf = x.astype(jnp.float32)
    mu = jnp.mean(xf, axis=-1, keepdims=True)
    xc = xf - mu
    y = xc * lax.rsqrt(jnp.mean(xc * xc, axis=-1, keepdims=True) + LN_EPS)
    return (y * g.astype(jnp.float32) + b.astype(jnp.float32)).astype(x.dtype)


def chunked_spatial_gating(h, w_in, ln_g, ln_b, w_s, b_s, w_out):
    B, S, _ = h.shape
    z = jax.nn.gelu(h @ w_in, approximate=False)
    u, v = jnp.split(z, 2, axis=-1)
    v = layernorm(v, ln_g, ln_b)
    v = v.reshape(B, S // CHUNK, CHUNK, N_GROUPS, GROUP_DIM)
    w = w_s * jnp.tril(jnp.ones((CHUNK, CHUNK), dtype=w_s.dtype))
    s = jnp.einsum('gts,bcsgd->bctgd', w, v) + b_s.T[:, :, None]
    return (u * s.reshape(B, S, SGU_WIDTH)) @ w_out


def stick_breaking_attention(h, w_qkv, w_o):
    B, S, _ = h.shape
    q, k, v = jnp.split(h @ w_qkv, 3, axis=-1)
    to_heads = lambda t: t.reshape(B, S, N_HEADS, HEAD_DIM).transpose(0, 2, 1, 3)
    q, k, v = to_heads(q), to_heads(k), to_heads(v)
    n_blocks = S // Q_BLOCK
    scale = HEAD_DIM ** -0.5
    idx = jnp.arange(Q_BLOCK)
    strict_upper = (idx[:, None] > idx[None, :]).astype(jnp.float32)
    outs = []
    for blk in range(n_blocks):
        n_kb = blk + 1
        L = n_kb * Q_BLOCK
        qb = q[:, :, blk * Q_BLOCK:(blk + 1) * Q_BLOCK]
        z = jnp.einsum('bhqd,bhkd->bhqk', qb, k[:, :, :L]).astype(jnp.float32) * scale
        mask = jnp.arange(L)[None, :] < (blk * Q_BLOCK + idx)[:, None]
        log_keep = jnp.where(mask, jax.nn.log_sigmoid(-z), 0.0)
        lk = log_keep.reshape(B, N_HEADS, Q_BLOCK, n_kb, Q_BLOCK)
        within = jnp.einsum('bhqcj,js->bhqcs', lk, strict_upper)
        blk_tot = jnp.sum(lk, axis=-1)
        later = lax.cumsum(blk_tot, axis=3, reverse=True) - blk_tot
        log_survive = (within + later[..., None]).reshape(B, N_HEADS, Q_BLOCK, L)
        a = jnp.where(mask, jnp.exp(jax.nn.log_sigmoid(z) + log_survive), 0.0)
        outs.append(jnp.einsum('bhqk,bhkd->bhqd', a.astype(v.dtype), v[:, :, :L]))
    out = jnp.concatenate(outs, axis=2)
    out = out.transpose(0, 2, 1, 3).reshape(B, S, D_MODEL)
    return out @ w_o


def swiglu(h, w_gate_up, w_down):
    gate, up = jnp.split(h @ w_gate_up, 2, axis=-1)
    return (jax.nn.silu(gate) * up) @ w_down


def setup_inputs(seed: int = 0) -> dict:
    key = jax.random.key(seed)
    ks = jax.random.split(key, 16)
    nrm = lambda k, shape, fan_in: jax.random.normal(k, shape, jnp.float32) * (fan_in ** -0.5)
    gain = lambda k, shape: 1.0 + 0.02 * jax.random.normal(k, shape, jnp.float32)
    return {
        "x": jax.random.normal(ks[0], (BATCH, SEQ, D_MODEL), jnp.float32),
        "norm_mix": gain(ks[1], (DEPTH, D_MODEL)),
        "a_w_in": nrm(ks[2], (N_A_LAYERS, D_MODEL, 2 * SGU_WIDTH), D_MODEL),
        "a_ln_g": gain(ks[3], (N_A_LAYERS, SGU_WIDTH)),
        "a_ln_b": 0.02 * jax.random.normal(ks[4], (N_A_LAYERS, SGU_WIDTH), jnp.float32),
        "a_w_s": nrm(ks[5], (N_A_LAYERS, N_GROUPS, CHUNK, CHUNK), CHUNK),
        "a_b_s": gain(ks[6], (N_A_LAYERS, N_GROUPS, CHUNK)),
        "a_w_out": nrm(ks[7], (N_A_LAYERS, SGU_WIDTH, D_MODEL), SGU_WIDTH),
        "b_w_qkv": nrm(ks[8], (N_B_LAYERS, D_MODEL, 3 * D_MODEL), D_MODEL),
        "b_w_o": nrm(ks[9], (N_B_LAYERS, D_MODEL, D_MODEL), D_MODEL),
        "norm_ffn": gain(ks[10], (DEPTH, D_MODEL)),
        "ffn_w_gate_up": nrm(ks[11], (DEPTH, D_MODEL, 2 * D_FF), D_MODEL),
        "ffn_w_down": nrm(ks[12], (DEPTH, D_FF, D_MODEL), D_FF),
        "final_norm": gain(ks[13], (D_MODEL,)),
    }


def reference(x, norm_mix, a_w_in, a_ln_g, a_ln_b, a_w_s, a_b_s, a_w_out,
              b_w_qkv, b_w_o, norm_ffn, ffn_w_gate_up, ffn_w_down, final_norm):
    for i in range(DEPTH):
        j = i // N_MIXERS
        h = rmsnorm(x, norm_mix[i])
        if i % N_MIXERS == 0:
            y = chunked_spatial_gating(h, a_w_in[j], a_ln_g[j], a_ln_b[j], a_w_s[j], a_b_s[j], a_w_out[j])
        else:
            y = stick_breaking_attention(h, b_w_qkv[j], b_w_o[j])
        x = x + y
        x = x + swiglu(rmsnorm(x, norm_ffn[i]), ffn_w_gate_up[i], ffn_w_down[i])
    return rmsnorm(x, final_norm)
```

```python
import functools

import jax
import jax.numpy as jnp
from jax import lax
from jax.experimental import pallas as pl
from jax.experimental.pallas import tpu as pltpu

NORM_EPS = 1e-6
LN_EPS = 1e-5
CHUNK = 128
GROUP_DIM = 128
HEAD_DIM = 256
FF_ALIGN = 1024

EXP_ZERO_F32 = 104.0

V7X_VMEM_LIMIT_BYTES = 56 * 1024 * 1024

BF16 = jnp.bfloat16
F32 = jnp.float32


def _params(n_axes):
    return pltpu.CompilerParams(
        dimension_semantics=("arbitrary",) * n_axes,
        vmem_limit_bytes=V7X_VMEM_LIMIT_BYTES,
    )


def _tile(dim, pref):
    t = min(dim, pref)
    while dim % t:
        t -= 128
    return t


def _rmsnorm_kernel(x_ref, g_ref, o_ref):
    x = x_ref[...]
    ms = jnp.mean(x * x, axis=-1, keepdims=True)
    o_ref[...] = (x * lax.rsqrt(ms + NORM_EPS) * g_ref[...]).astype(o_ref.dtype)


def rmsnorm(x, g, out_dtype):
    s, d = x.shape
    tm = _tile(s, 256)
    return pl.pallas_call(
        _rmsnorm_kernel,
        grid=(s // tm,),
        in_specs=[pl.BlockSpec((tm, d), lambda i: (i, 0)),
                  pl.BlockSpec((1, d), lambda i: (0, 0))],
        out_specs=pl.BlockSpec((tm, d), lambda i: (i, 0)),
        out_shape=jax.ShapeDtypeStruct((s, d), out_dtype),
        compiler_params=_params(1),
        name="rmsnorm",
    )(x, g.reshape(1, d))


def _gelu_exact(x):
    return 0.5 * x * (1.0 + lax.erf(x * (2.0 ** -0.5)))


def _mm_kernel(a_ref, w_ref, o_ref, *, act):
    acc = jnp.dot(a_ref[...], w_ref[...], preferred_element_type=F32)
    if act == "gelu":
        acc = _gelu_exact(acc)
    o_ref[...] = acc.astype(o_ref.dtype)


def matmul(a, w, *, act=None, tm=1024, tn=1024):
    m, k = a.shape
    _, n = w.shape
    tm, tn = _tile(m, tm), _tile(n, tn)
    return pl.pallas_call(
        functools.partial(_mm_kernel, act=act),
        grid=(m // tm, n // tn),
        in_specs=[pl.BlockSpec((tm, k), lambda i, j: (i, 0)),
                  pl.BlockSpec((k, tn), lambda i, j: (0, j))],
        out_specs=pl.BlockSpec((tm, tn), lambda i, j: (i, j)),
        out_shape=jax.ShapeDtypeStruct((m, n), BF16),
        compiler_params=_params(2),
        name="matmul_" + (act or "plain"),
    )(a, w)


def _mm_res_kernel(a_ref, w_ref, r_ref, o_ref):
    kk = pl.program_id(2)
    part = jnp.dot(a_ref[...], w_ref[...], preferred_element_type=F32)

    @pl.when(kk == 0)
    def _():
        o_ref[...] = r_ref[...] + part

    @pl.when(kk != 0)
    def _():
        o_ref[...] += part


def matmul_residual(a, w, res, *, tm=1024, tn=1024, tk=4096):
    m, k = a.shape
    _, n = w.shape
    tm, tn, tk = _tile(m, tm), _tile(n, tn), _tile(k, tk)
    return pl.pallas_call(
        _mm_res_kernel,
        grid=(m // tm, n // tn, k // tk),
        in_specs=[pl.BlockSpec((tm, tk), lambda i, j, kk: (i, kk)),
                  pl.BlockSpec((tk, tn), lambda i, j, kk: (kk, j)),
                  pl.BlockSpec((tm, tn), lambda i, j, kk: (i, j))],
        out_specs=pl.BlockSpec((tm, tn), lambda i, j, kk: (i, j)),
        out_shape=jax.ShapeDtypeStruct((m, n), F32),
        compiler_params=_params(3),
        name="matmul_residual",
    )(a, w, res)


def _swiglu_kernel(a_ref, wg_ref, wu_ref, o_ref):
    a = a_ref[...]
    gate = jnp.dot(a, wg_ref[...], preferred_element_type=F32)
    up = jnp.dot(a, wu_ref[...], preferred_element_type=F32)
    o_ref[...] = (gate * jax.nn.sigmoid(gate) * up).astype(o_ref.dtype)


def swiglu(a, wg, wu, *, tm=1024, tn=512):
    m, k = a.shape
    _, f = wg.shape
    tm, tn = _tile(m, tm), _tile(f, tn)
    return pl.pallas_call(
        _swiglu_kernel,
        grid=(m // tm, f // tn),
        in_specs=[pl.BlockSpec((tm, k), lambda i, j: (i, 0)),
                  pl.BlockSpec((k, tn), lambda i, j: (0, j)),
                  pl.BlockSpec((k, tn), lambda i, j: (0, j))],
        out_specs=pl.BlockSpec((tm, tn), lambda i, j: (i, j)),
        out_shape=jax.ShapeDtypeStruct((m, f), BF16),
        compiler_params=_params(2),
        name="swiglu",
    )(a, wg, wu)


def _sgu_kernel(u_ref, v_ref, g_ref, b_ref, ws_ref, bst_ref, o_ref, *, n_chunks, n_groups):
    v = v_ref[...].astype(F32)
    mu = jnp.mean(v, axis=-1, keepdims=True)
    vc = v - mu
    var = jnp.mean(vc * vc, axis=-1, keepdims=True)
    vn = (vc * lax.rsqrt(var + LN_EPS) * g_ref[...] + b_ref[...]).astype(BF16)

    row = lax.broadcasted_iota(jnp.int32, (CHUNK, CHUNK), 0)
    col = lax.broadcasted_iota(jnp.int32, (CHUNK, CHUNK), 1)
    causal = row >= col
    for g in range(n_groups):
        cols = slice(g * GROUP_DIM, (g + 1) * GROUP_DIM)
        w = jnp.where(causal, ws_ref[g], 0.0).astype(BF16)
        bias = bst_ref[:, g:g + 1]
        for c in range(n_chunks):
            rows = slice(c * CHUNK, (c + 1) * CHUNK)
            s = jnp.dot(w, vn[rows, cols], preferred_element_type=F32) + bias
            o_ref[rows, cols] = (u_ref[rows, cols].astype(F32) * s).astype(o_ref.dtype)


def spatial_gating(z, ln_g, ln_b, w_s, b_s, *, tm=256):
    s, w2 = z.shape
    wdt = w2 // 2
    n_groups = wdt // GROUP_DIM
    tm = _tile(s, tm)
    assert tm % CHUNK == 0 and w_s.shape == (n_groups, CHUNK, CHUNK)
    return pl.pallas_call(
        functools.partial(_sgu_kernel, n_chunks=tm // CHUNK, n_groups=n_groups),
        grid=(s // tm,),
        in_specs=[pl.BlockSpec((tm, wdt), lambda i: (i, 0)),
                  pl.BlockSpec((tm, wdt), lambda i: (i, 1)),
                  pl.BlockSpec((1, wdt), lambda i: (0, 0)),
                  pl.BlockSpec((1, wdt), lambda i: (0, 0)),
                  pl.BlockSpec((n_groups, CHUNK, CHUNK), lambda i: (0, 0, 0)),
                  pl.BlockSpec((CHUNK, n_groups), lambda i: (0, 0))],
        out_specs=pl.BlockSpec((tm, wdt), lambda i: (i, 0)),
        out_shape=jax.ShapeDtypeStruct((s, wdt), BF16),
        compiler_params=_params(1),
        name="spatial_gating",
    )(z, z, ln_g.reshape(1, wdt), ln_b.reshape(1, wdt), w_s, b_s.T)


def _attn_kernel(q_ref, k_ref, v_ref, o_ref, acc_ref, run_ref, *, t, scale):
    qi = pl.program_id(1)
    q = q_ref[...]
    row = lax.broadcasted_iota(jnp.int32, (t, t), 0)
    col = lax.broadcasted_iota(jnp.int32, (t, t), 1)
    suffix_and_total = jnp.concatenate(
        [(row > col).astype(BF16), jnp.ones((t, t), BF16)], axis=1)

    def visit(kt, diagonal):
        start = pl.multiple_of(kt * t, t)
        k = k_ref[pl.ds(start, t), :]
        v = v_ref[pl.ds(start, t), :]
        z = lax.dot_general(q, k, (((1,), (1,)), ((), ())), preferred_element_type=F32) * scale
        sp = jnp.maximum(z, 0.0) + jnp.log1p(jnp.exp(-jnp.abs(z)))
        if diagonal:
            valid = col < row
            sp = jnp.where(valid, sp, 0.0)
        hi = sp.astype(BF16)
        lo = (sp - hi.astype(F32)).astype(BF16)
        sums = (jnp.dot(hi, suffix_and_total, preferred_element_type=F32)
                + jnp.dot(lo, suffix_and_total, preferred_element_type=F32))
        run = run_ref[...]
        a = jnp.exp(z - sp - sums[:, :t] - run)
        if diagonal:
            a = jnp.where(valid, a, 0.0)
        acc_ref[...] += jnp.dot(a.astype(BF16), v, preferred_element_type=F32)
        run_ref[...] = run + sums[:, t:]

    acc_ref[...] = jnp.zeros_like(acc_ref)
    run_ref[...] = jnp.zeros_like(run_ref)
    visit(qi, True)

    def more(kt):
        return jnp.logical_and(kt >= 0, jnp.min(run_ref[...]) < EXP_ZERO_F32)

    def step(kt):
        visit(kt, False)
        return kt - 1

    lax.while_loop(more, step, qi - 1)
    o_ref[...] = acc_ref[...].astype(o_ref.dtype)


def stick_breaking_attention(qkv, *, t=256):
    s, d3 = qkv.shape
    d = d3 // 3
    n_heads = d // HEAD_DIM
    t = _tile(s, t)
    assert HEAD_DIM == t or t % 128 == 0
    return pl.pallas_call(
        functools.partial(_attn_kernel, t=t, scale=HEAD_DIM ** -0.5),
        grid=(n_heads, s // t),
        in_specs=[pl.BlockSpec((t, HEAD_DIM), lambda h, i: (i, h)),
                  pl.BlockSpec((s, HEAD_DIM), lambda h, i: (0, n_heads + h)),
                  pl.BlockSpec((s, HEAD_DIM), lambda h, i: (0, 2 * n_heads + h))],
        out_specs=pl.BlockSpec((t, HEAD_DIM), lambda h, i: (i, h)),
        out_shape=jax.ShapeDtypeStruct((s, d), BF16),
        scratch_shapes=[pltpu.VMEM((t, HEAD_DIM), F32),
                        pltpu.VMEM((t, t), F32)],
        compiler_params=_params(2),
        name="stick_breaking_attention",
    )(qkv, qkv, qkv)


def _pad_to(w, axis, mult):
    pad = (-w.shape[axis]) % mult
    if pad == 0:
        return w
    widths = [(0, 0)] * w.ndim
    widths[axis] = (0, pad)
    return jnp.pad(w, widths)


def kernel(x, norm_mix, a_w_in, a_ln_g, a_ln_b, a_w_s, a_b_s, a_w_out, b_w_qkv, b_w_o,
           norm_ffn, ffn_w_gate_up, ffn_w_down, final_norm):
    b, s, d = x.shape
    assert b == 1, "kernels treat the sequence as one batch row"
    depth = norm_mix.shape[0]
    d_ff = ffn_w_down.shape[1]
    xs = x.reshape(s, d)
    for i in range(depth):
        j = i // 2
        h = rmsnorm(xs, norm_mix[i], BF16)
        if i % 2 == 0:
            z = matmul(h, a_w_in[j].astype(BF16), act="gelu")
            y = spatial_gating(z, a_ln_g[j], a_ln_b[j], a_w_s[j], a_b_s[j])
            xs = matmul_residual(y, a_w_out[j].astype(BF16), xs)
        else:
            qkv = matmul(h, b_w_qkv[j].astype(BF16))
            o = stick_breaking_attention(qkv)
            xs = matmul_residual(o, b_w_o[j].astype(BF16), xs)
        h = rmsnorm(xs, norm_ffn[i], BF16)
        w_gu = ffn_w_gate_up[i]
        wg = _pad_to(w_gu[:, :d_ff].astype(BF16), 1, FF_ALIGN)
        wu = _pad_to(w_gu[:, d_ff:].astype(BF16), 1, FF_ALIGN)
        wd = _pad_to(ffn_w_down[i].astype(BF16), 0, FF_ALIGN)
        hid = swiglu(h, wg, wu)
        xs = matmul_residual(hid, wd, xs, tk=2816)
    out = rmsnorm(xs, final_norm, F32)
    return out.reshape(b, s, d)
```

```python
import functools

import jax
import jax.numpy as jnp
from jax import lax
from jax.experimental import pallas as pl
from jax.experimental.pallas import tpu as pltpu

NORM_EPS = 1e-6
LN_EPS = 1e-5
CHUNK = 128
GROUP_DIM = 128
HEAD_DIM = 256
FF_TILE = 256
FF_ALIGN = 1024

EXP_ZERO_F32 = 104.0

V7X_VMEM_LIMIT_BYTES = 56 * 1024 * 1024

BF16 = jnp.bfloat16
F32 = jnp.float32


def _params(n_axes):
    return pltpu.CompilerParams(
        dimension_semantics=("arbitrary",) * n_axes,
        vmem_limit_bytes=V7X_VMEM_LIMIT_BYTES,
    )


def _tile(dim, pref):
    t = min(dim, pref)
    while dim % t:
        t -= 128
    return t


def _row_scale(ssq, width):
    return lax.rsqrt(ssq * (1.0 / width) + NORM_EPS)


def _gain_sumsq_kernel(x_ref, g_ref, xg_ref, ssq_ref):
    x = x_ref[...]
    xg_ref[...] = (x * g_ref[...]).astype(xg_ref.dtype)
    ssq_ref[...] = jnp.sum(x * x, axis=-1, keepdims=True)


def gain_sumsq(x, g):
    s, d = x.shape
    tm = _tile(s, 256)
    return pl.pallas_call(
        _gain_sumsq_kernel,
        grid=(s // tm,),
        in_specs=[pl.BlockSpec((tm, d), lambda i: (i, 0)),
                  pl.BlockSpec((1, d), lambda i: (0, 0))],
        out_specs=[pl.BlockSpec((tm, d), lambda i: (i, 0)),
                   pl.BlockSpec((tm, 1), lambda i: (i, 0))],
        out_shape=[jax.ShapeDtypeStruct((s, d), BF16),
                   jax.ShapeDtypeStruct((s, 1), F32)],
        compiler_params=_params(1),
        name="gain_sumsq",
    )(x, g.reshape(1, d))


def _rmsnorm_kernel(x_ref, g_ref, o_ref):
    x = x_ref[...]
    ms = jnp.mean(x * x, axis=-1, keepdims=True)
    o_ref[...] = (x * lax.rsqrt(ms + NORM_EPS) * g_ref[...]).astype(o_ref.dtype)


def rmsnorm(x, g, out_dtype):
    s, d = x.shape
    tm = _tile(s, 256)
    return pl.pallas_call(
        _rmsnorm_kernel,
        grid=(s // tm,),
        in_specs=[pl.BlockSpec((tm, d), lambda i: (i, 0)),
                  pl.BlockSpec((1, d), lambda i: (0, 0))],
        out_specs=pl.BlockSpec((tm, d), lambda i: (i, 0)),
        out_shape=jax.ShapeDtypeStruct((s, d), out_dtype),
        compiler_params=_params(1),
        name="rmsnorm",
    )(x, g.reshape(1, d))


def _gelu_exact(x):
    return 0.5 * x * (1.0 + lax.erf(x * (2.0 ** -0.5)))


def _normed_mm_kernel(a_ref, ssq_ref, w_ref, o_ref, *, act):
    acc = jnp.dot(a_ref[...], w_ref[...], preferred_element_type=F32)
    acc = acc * _row_scale(ssq_ref[...], a_ref.shape[1])
    if act == "gelu":
        acc = _gelu_exact(acc)
    o_ref[...] = acc.astype(o_ref.dtype)


def normed_matmul(xg, ssq, w_stack, layer, *, act=None, tm=1024, tn=1024):
    m, k = xg.shape
    _, _, n = w_stack.shape
    tm, tn = _tile(m, tm), _tile(n, tn)
    return pl.pallas_call(
        functools.partial(_normed_mm_kernel, act=act),
        grid=(m // tm, n // tn),
        in_specs=[pl.BlockSpec((tm, k), lambda i, j: (i, 0)),
                  pl.BlockSpec((tm, 1), lambda i, j: (i, 0)),
                  pl.BlockSpec((None, k, tn), lambda i, j: (layer, 0, j))],
        out_specs=pl.BlockSpec((tm, tn), lambda i, j: (i, j)),
        out_shape=jax.ShapeDtypeStruct((m, n), BF16),
        compiler_params=_params(2),
        name="normed_matmul_" + (act or "plain"),
    )(xg, ssq, w_stack)


def _normed_swiglu_kernel(a_ref, ssq_ref, wg0_ref, wg1_ref, wu0_ref, wu1_ref, o_ref, *, n_valid):
    a = a_ref[...]
    r = _row_scale(ssq_ref[...], a_ref.shape[1])

    def half(wg_ref, wu_ref):
        gate = jnp.dot(a, wg_ref[...], preferred_element_type=F32) * r
        up = jnp.dot(a, wu_ref[...], preferred_element_type=F32) * r
        return (gate * jax.nn.sigmoid(gate) * up).astype(o_ref.dtype)

    o_ref[:, :FF_TILE] = half(wg0_ref, wu0_ref)
    second_valid = 2 * pl.program_id(1) + 1 < n_valid
    o_ref[:, FF_TILE:] = jnp.where(second_valid, half(wg1_ref, wu1_ref), jnp.zeros((), o_ref.dtype))


def normed_swiglu(xg, ssq, w_stack, layer, d_ff, d_ff_pad, *, tm=1024):
    m, k = xg.shape
    assert d_ff % FF_TILE == 0 and d_ff_pad % (2 * FF_TILE) == 0 and d_ff_pad - d_ff < 2 * FF_TILE
    n_valid = d_ff // FF_TILE
    tm = _tile(m, tm)

    def w_spec(first_tile, which):
        return pl.BlockSpec((None, k, FF_TILE),
                            lambda i, j: (layer, 0, first_tile + jnp.minimum(2 * j + which, n_valid - 1)))

    return pl.pallas_call(
        functools.partial(_normed_swiglu_kernel, n_valid=n_valid),
        grid=(m // tm, d_ff_pad // (2 * FF_TILE)),
        in_specs=[pl.BlockSpec((tm, k), lambda i, j: (i, 0)),
                  pl.BlockSpec((tm, 1), lambda i, j: (i, 0)),
                  w_spec(0, 0), w_spec(0, 1), w_spec(n_valid, 0), w_spec(n_valid, 1)],
        out_specs=pl.BlockSpec((tm, 2 * FF_TILE), lambda i, j: (i, j)),
        out_shape=jax.ShapeDtypeStruct((m, d_ff_pad), BF16),
        compiler_params=_params(2),
        name="normed_swiglu",
    )(xg, ssq, w_stack, w_stack, w_stack, w_stack)


def _mm_res_kernel(a_ref, w_ref, r_ref, g_ref, o_ref, *norm_refs):
    j, kk = pl.program_id(1), pl.program_id(2)
    part = jnp.dot(a_ref[...], w_ref[...], preferred_element_type=F32)

    @pl.when(kk == 0)
    def _():
        o_ref[...] = r_ref[...] + part

    @pl.when(kk != 0)
    def _():
        o_ref[...] += part

    if not norm_refs:
        return
    xg_ref, ssq_ref = norm_refs

    @pl.when(kk == pl.num_programs(2) - 1)
    def _():
        xn = o_ref[...]
        xg_ref[...] = (xn * g_ref[...]).astype(xg_ref.dtype)
        s = jnp.sum(xn * xn, axis=-1, keepdims=True)

        @pl.when(j == 0)
        def _():
            ssq_ref[...] = s

        @pl.when(j != 0)
        def _():
            ssq_ref[...] += s


def matmul_residual(a, w_stack, layer, res, g_next, *, emit_norm=True, tm=1024, tn=1024, tk=4096):
    m, k = a.shape
    _, _, n = w_stack.shape
    tm, tn, tk = _tile(m, tm), _tile(n, tn), _tile(k, tk)
    out_specs = [pl.BlockSpec((tm, tn), lambda i, j, kk: (i, j))]
    out_shape = [jax.ShapeDtypeStruct((m, n), F32)]
    if emit_norm:
        out_specs += [pl.BlockSpec((tm, tn), lambda i, j, kk: (i, j)),
                      pl.BlockSpec((tm, 1), lambda i, j, kk: (i, 0))]
        out_shape += [jax.ShapeDtypeStruct((m, n), BF16), jax.ShapeDtypeStruct((m, 1), F32)]
    return pl.pallas_call(
        _mm_res_kernel,
        grid=(m // tm, n // tn, k // tk),
        in_specs=[pl.BlockSpec((tm, tk), lambda i, j, kk: (i, kk)),
                  pl.BlockSpec((None, tk, tn), lambda i, j, kk: (layer, kk, j)),
                  pl.BlockSpec((tm, tn), lambda i, j, kk: (i, j)),
                  pl.BlockSpec((1, tn), lambda i, j, kk: (0, j))],
        out_specs=out_specs,
        out_shape=out_shape,
        compiler_params=_params(3),
        name="matmul_residual",
    )(a, w_stack, res, g_next.reshape(1, n))


def _sgu_kernel(u_ref, v_ref, g_ref, b_ref, ws_ref, bst_ref, o_ref, *, n_chunks, n_groups):
    v = v_ref[...].astype(F32)
    mu = jnp.mean(v, axis=-1, keepdims=True)
    vc = v - mu
    var = jnp.mean(vc * vc, axis=-1, keepdims=True)
    vn = (vc * lax.rsqrt(var + LN_EPS) * g_ref[...] + b_ref[...]).astype(BF16)

    row = lax.broadcasted_iota(jnp.int32, (CHUNK, CHUNK), 0)
    col = lax.broadcasted_iota(jnp.int32, (CHUNK, CHUNK), 1)
    causal = row >= col
    for g in range(n_groups):
        cols = slice(g * GROUP_DIM, (g + 1) * GROUP_DIM)
        w = jnp.where(causal, ws_ref[g], 0.0).astype(BF16)
        bias = bst_ref[:, g:g + 1]
        for c in range(n_chunks):
            rows = slice(c * CHUNK, (c + 1) * CHUNK)
            s = jnp.dot(w, vn[rows, cols], preferred_element_type=F32) + bias
            o_ref[rows, cols] = (u_ref[rows, cols].astype(F32) * s).astype(o_ref.dtype)


def spatial_gating(z, ln_g, ln_b, w_s, b_s, *, tm=256):
    s, w2 = z.shape
    wdt = w2 // 2
    n_groups = wdt // GROUP_DIM
    tm = _tile(s, tm)
    assert tm % CHUNK == 0 and w_s.shape == (n_groups, CHUNK, CHUNK)
    return pl.pallas_call(
        functools.partial(_sgu_kernel, n_chunks=tm // CHUNK, n_groups=n_groups),
        grid=(s // tm,),
        in_specs=[pl.BlockSpec((tm, wdt), lambda i: (i, 0)),
                  pl.BlockSpec((tm, wdt), lambda i: (i, 1)),
                  pl.BlockSpec((1, wdt), lambda i: (0, 0)),
                  pl.BlockSpec((1, wdt), lambda i: (0, 0)),
                  pl.BlockSpec((n_groups, CHUNK, CHUNK), lambda i: (0, 0, 0)),
                  pl.BlockSpec((CHUNK, n_groups), lambda i: (0, 0))],
        out_specs=pl.BlockSpec((tm, wdt), lambda i: (i, 0)),
        out_shape=jax.ShapeDtypeStruct((s, wdt), BF16),
        compiler_params=_params(1),
        name="spatial_gating",
    )(z, z, ln_g.reshape(1, wdt), ln_b.reshape(1, wdt), w_s, b_s.T)


def _attn_kernel(q_ref, k_ref, v_ref, o_ref, acc_ref, run_ref, *, t, scale):
    blk = pl.program_id(1)
    row = lax.broadcasted_iota(jnp.int32, (t, t), 0)
    col = lax.broadcasted_iota(jnp.int32, (t, t), 1)
    later = (row > col).astype(BF16)
    strictly_causal = col < row

    def visit(q, kt, run, keep):
        start = pl.multiple_of(kt * t, t)
        k = k_ref[pl.ds(start, t), :]
        v = v_ref[pl.ds(start, t), :]
        z = lax.dot_general(q, k, (((1,), (1,)), ((), ())), preferred_element_type=F32) * scale
        sp = jnp.maximum(z, 0.0) + jnp.log(1.0 + jnp.exp(-jnp.abs(z)))
        if keep is not None:
            sp = jnp.where(keep, sp, 0.0)
        hi = sp.astype(BF16)
        lo = (sp - hi.astype(F32)).astype(BF16)
        within = (jnp.dot(hi, later, preferred_element_type=F32)
                  + jnp.dot(lo, later, preferred_element_type=F32))
        a = jnp.exp(z - sp - within - run)
        if keep is not None:
            a = jnp.where(keep, a, 0.0)
        pv = jnp.dot(a.astype(BF16), v, preferred_element_type=F32)
        return pv, within[:, :1] + sp[:, :1]

    for half in range(2):
        q = q_ref[half * t:(half + 1) * t, :]
        kd = 2 * blk + half
        pv_d, tot_d = visit(q, kd, 0.0, strictly_causal)
        has_prev = None if half else blk > 0
        pv_p, tot_p = visit(q, jnp.maximum(kd - 1, 0), tot_d, has_prev)
        acc_ref[half] = pv_d + pv_p
        run_ref[half] = tot_d + tot_p

    def more(n):
        return jnp.logical_and(2 * blk - 1 - n >= 0, jnp.min(run_ref[...]) < EXP_ZERO_F32)

    def step(n):
        for half in range(2):
            kt = 2 * blk - 2 + half - n
            keep = None if half else kt >= 0
            pv, tot = visit(q_ref[half * t:(half + 1) * t, :], jnp.maximum(kt, 0), run_ref[half], keep)
            acc_ref[half] += pv
            run_ref[half] += tot
        return n + 1

    lax.while_loop(more, step, 0)
    for half in range(2):
        o_ref[half * t:(half + 1) * t, :] = acc_ref[half].astype(o_ref.dtype)


def stick_breaking_attention(qkv, *, t=256):
    s, d3 = qkv.shape
    d = d3 // 3
    n_heads = d // HEAD_DIM
    t = _tile(s // 2, t)
    return pl.pallas_call(
        functools.partial(_attn_kernel, t=t, scale=HEAD_DIM ** -0.5),
        grid=(n_heads, s // (2 * t)),
        in_specs=[pl.BlockSpec((2 * t, HEAD_DIM), lambda h, i: (i, h)),
                  pl.BlockSpec((s, HEAD_DIM), lambda h, i: (0, n_heads + h)),
                  pl.BlockSpec((s, HEAD_DIM), lambda h, i: (0, 2 * n_heads + h))],
        out_specs=pl.BlockSpec((2 * t, HEAD_DIM), lambda h, i: (i, h)),
        out_shape=jax.ShapeDtypeStruct((s, d), BF16),
        scratch_shapes=[pltpu.VMEM((2, t, HEAD_DIM), F32),
                        pltpu.VMEM((2, t, 1), F32)],
        compiler_params=_params(2),
        name="stick_breaking_attention",
    )(qkv, qkv, qkv)


def kernel(x, norm_mix, a_w_in, a_ln_g, a_ln_b, a_w_s, a_b_s, a_w_out, b_w_qkv, b_w_o,
           norm_ffn, ffn_w_gate_up, ffn_w_down, final_norm):
    b, s, d = x.shape
    assert b == 1, "kernels treat the sequence as one batch row"
    depth = norm_mix.shape[0]
    d_ff = ffn_w_down.shape[1]
    d_ff_pad = -(-d_ff // FF_ALIGN) * FF_ALIGN
    w_in, w_out, w_qkv, w_o, w_gate_up = (w.astype(BF16) for w in (a_w_in, a_w_out, b_w_qkv, b_w_o, ffn_w_gate_up))
    w_down = jnp.pad(ffn_w_down.astype(BF16), ((0, 0), (0, d_ff_pad - d_ff), (0, 0)))
    xs = x.reshape(s, d)
    xg, ssq = gain_sumsq(xs, norm_mix[0])
    for i in range(depth):
        j = i // 2
        if i % 2 == 0:
            z = normed_matmul(xg, ssq, w_in, j, act="gelu")
            y = spatial_gating(z, a_ln_g[j], a_ln_b[j], a_w_s[j], a_b_s[j])
            w_mix_out = w_out
        else:
            qkv = normed_matmul(xg, ssq, w_qkv, j)
            y = stick_breaking_attention(qkv)
            w_mix_out = w_o
        xs, xg, ssq = matmul_residual(y, w_mix_out, j, xs, norm_ffn[i], tn=512)
        hid = normed_swiglu(xg, ssq, w_gate_up, i, d_ff, d_ff_pad)
        if i + 1 < depth:
            xs, xg, ssq = matmul_residual(hid, w_down, i, xs, norm_mix[i + 1], tk=d_ff_pad // 4)
        else:
            (xs,) = matmul_residual(hid, w_down, i, xs, final_norm, emit_norm=False, tk=d_ff_pad // 4)
    out = rmsnorm(xs, final_norm, F32)
    return out.reshape(b, s, d)
```

```python
import functools

import jax
import jax.numpy as jnp
from jax import lax
from jax.experimental import pallas as pl
from jax.experimental.pallas import tpu as pltpu

NORM_EPS = 1e-6
LN_EPS = 1e-5
CHUNK = 128
GROUP_DIM = 128
HEAD_DIM = 256
FF_TILE = 256
DOWN_K_TILES = 4

EXP_ZERO_F32 = 104.0

V7X_VMEM_LIMIT_BYTES = 56 * 1024 * 1024

BF16 = jnp.bfloat16
F32 = jnp.float32


def _params(n_axes):
    return pltpu.CompilerParams(
        dimension_semantics=("arbitrary",) * n_axes,
        vmem_limit_bytes=V7X_VMEM_LIMIT_BYTES,
    )


def _tile(dim, pref):
    t = min(dim, pref)
    while dim % t:
        t -= 128
    return t


def _row_scale(ssq, width):
    return lax.rsqrt(ssq * (1.0 / width) + NORM_EPS)


def _gain_sumsq_kernel(x_ref, g_ref, xg_ref, ssq_ref):
    x = x_ref[...]
    xg_ref[...] = (x * g_ref[...]).astype(xg_ref.dtype)
    ssq_ref[...] = jnp.sum(x * x, axis=-1, keepdims=True)


def gain_sumsq(x, g):
    s, d = x.shape
    tm = _tile(s, 256)
    return pl.pallas_call(
        _gain_sumsq_kernel,
        grid=(s // tm,),
        in_specs=[pl.BlockSpec((tm, d), lambda i: (i, 0)),
                  pl.BlockSpec((1, d), lambda i: (0, 0))],
        out_specs=[pl.BlockSpec((tm, d), lambda i: (i, 0)),
                   pl.BlockSpec((tm, 1), lambda i: (i, 0))],
        out_shape=[jax.ShapeDtypeStruct((s, d), BF16),
                   jax.ShapeDtypeStruct((s, 1), F32)],
        compiler_params=_params(1),
        name="gain_sumsq",
    )(x, g.reshape(1, d))


def _rmsnorm_kernel(x_ref, g_ref, o_ref):
    x = x_ref[...]
    ms = jnp.mean(x * x, axis=-1, keepdims=True)
    o_ref[...] = (x * lax.rsqrt(ms + NORM_EPS) * g_ref[...]).astype(o_ref.dtype)


def rmsnorm(x, g, out_dtype):
    s, d = x.shape
    tm = _tile(s, 256)
    return pl.pallas_call(
        _rmsnorm_kernel,
        grid=(s // tm,),
        in_specs=[pl.BlockSpec((tm, d), lambda i: (i, 0)),
                  pl.BlockSpec((1, d), lambda i: (0, 0))],
        out_specs=pl.BlockSpec((tm, d), lambda i: (i, 0)),
        out_shape=jax.ShapeDtypeStruct((s, d), out_dtype),
        compiler_params=_params(1),
        name="rmsnorm",
    )(x, g.reshape(1, d))


def _gelu_exact(x):
    return 0.5 * x * (1.0 + lax.erf(x * (2.0 ** -0.5)))


def _normed_mm_kernel(a_ref, ssq_ref, w_ref, o_ref, *, act):
    acc = jnp.dot(a_ref[...], w_ref[...], preferred_element_type=F32)
    acc = acc * _row_scale(ssq_ref[...], a_ref.shape[1])
    if act == "gelu":
        acc = _gelu_exact(acc)
    o_ref[...] = acc.astype(o_ref.dtype)


def normed_matmul(xg, ssq, w_stack, layer, *, act=None, tm=1024, tn=1024):
    m, k = xg.shape
    _, _, n = w_stack.shape
    tm, tn = _tile(m, tm), _tile(n, tn)
    return pl.pallas_call(
        functools.partial(_normed_mm_kernel, act=act),
        grid=(m // tm, n // tn),
        in_specs=[pl.BlockSpec((tm, k), lambda i, j: (i, 0)),
                  pl.BlockSpec((tm, 1), lambda i, j: (i, 0)),
                  pl.BlockSpec((None, k, tn), lambda i, j: (layer, 0, j))],
        out_specs=pl.BlockSpec((tm, tn), lambda i, j: (i, j)),
        out_shape=jax.ShapeDtypeStruct((m, n), BF16),
        compiler_params=_params(2),
        name="normed_matmul_" + (act or "plain"),
    )(xg, ssq, w_stack)


def _normed_swiglu_kernel(a_ref, ssq_ref, wg0_ref, wg1_ref, wu0_ref, wu1_ref, o_ref):
    a = a_ref[...]
    r = _row_scale(ssq_ref[...], a_ref.shape[1])

    def half(wg_ref, wu_ref):
        gate = jnp.dot(a, wg_ref[...], preferred_element_type=F32) * r
        up = jnp.dot(a, wu_ref[...], preferred_element_type=F32) * r
        return (gate * jax.nn.sigmoid(gate) * up).astype(o_ref.dtype)

    o_ref[:, :FF_TILE] = half(wg0_ref, wu0_ref)
    o_ref[:, FF_TILE:] = half(wg1_ref, wu1_ref)


def normed_swiglu(xg, ssq, w_stack, layer, *, tm=1024):
    m, k = xg.shape
    d_ff = w_stack.shape[2] // 2
    assert d_ff % FF_TILE == 0
    n_tiles = d_ff // FF_TILE
    tm = _tile(m, tm)

    def w_spec(first_tile, which):
        return pl.BlockSpec((None, k, FF_TILE),
                            lambda i, j: (layer, 0, first_tile + jnp.minimum(2 * j + which, n_tiles - 1)))

    return pl.pallas_call(
        _normed_swiglu_kernel,
        grid=(m // tm, pl.cdiv(n_tiles, 2)),
        in_specs=[pl.BlockSpec((tm, k), lambda i, j: (i, 0)),
                  pl.BlockSpec((tm, 1), lambda i, j: (i, 0)),
                  w_spec(0, 0), w_spec(0, 1), w_spec(n_tiles, 0), w_spec(n_tiles, 1)],
        out_specs=pl.BlockSpec((tm, 2 * FF_TILE), lambda i, j: (i, j)),
        out_shape=jax.ShapeDtypeStruct((m, d_ff), BF16),
        compiler_params=_params(2),
        name="normed_swiglu",
    )(xg, ssq, w_stack, w_stack, w_stack, w_stack)


def _mm_res_kernel(a_ref, w_ref, r_ref, g_ref, o_ref, *norm_refs, nk, k_last):
    j, kk = pl.program_id(1), pl.program_id(2)

    def partial_product(k_used=None):
        return jnp.dot(a_ref[:, :k_used], w_ref[:k_used, :], preferred_element_type=F32)

    def emit_norm(xn):
        xg_ref, ssq_ref = norm_refs
        xg_ref[...] = (xn * g_ref[...]).astype(xg_ref.dtype)
        s = jnp.sum(xn * xn, axis=-1, keepdims=True)

        @pl.when(j == 0)
        def _():
            ssq_ref[...] = s

        @pl.when(j != 0)
        def _():
            ssq_ref[...] += s

    if nk == 1:
        xn = r_ref[...] + partial_product(k_last)
        o_ref[...] = xn
        if norm_refs:
            emit_norm(xn)
        return

    @pl.when(kk == 0)
    def _():
        o_ref[...] = r_ref[...] + partial_product()

    @pl.when(jnp.logical_and(kk != 0, kk != nk - 1))
    def _():
        o_ref[...] += partial_product()

    @pl.when(kk == nk - 1)
    def _():
        xn = o_ref[...] + partial_product(k_last)
        o_ref[...] = xn
        if norm_refs:
            emit_norm(xn)


def matmul_residual(a, w_stack, layer, res, g_next, *, emit_norm=True, tm=1024, tn=1024, tk=4096):
    m, k = a.shape
    _, _, n = w_stack.shape
    tm, tn, tk = _tile(m, tm), _tile(n, tn), min(tk, k)
    nk = pl.cdiv(k, tk)
    k_last = k - (nk - 1) * tk
    assert tk % 128 == 0 and k_last % 128 == 0
    out_specs = [pl.BlockSpec((tm, tn), lambda i, j, kk: (i, j))]
    out_shape = [jax.ShapeDtypeStruct((m, n), F32)]
    if emit_norm:
        out_specs += [pl.BlockSpec((tm, tn), lambda i, j, kk: (i, j)),
                      pl.BlockSpec((tm, 1), lambda i, j, kk: (i, 0))]
        out_shape += [jax.ShapeDtypeStruct((m, n), BF16), jax.ShapeDtypeStruct((m, 1), F32)]
    return pl.pallas_call(
        functools.partial(_mm_res_kernel, nk=nk, k_last=k_last),
        grid=(m // tm, n // tn, nk),
        in_specs=[pl.BlockSpec((tm, tk), lambda i, j, kk: (i, kk)),
                  pl.BlockSpec((None, tk, tn), lambda i, j, kk: (layer, kk, j)),
                  pl.BlockSpec((tm, tn), lambda i, j, kk: (i, j)),
                  pl.BlockSpec((1, tn), lambda i, j, kk: (0, j))],
        out_specs=out_specs,
        out_shape=out_shape,
        compiler_params=_params(3),
        name="matmul_residual",
    )(a, w_stack, res, g_next.reshape(1, n))


def _sgu_kernel(u_ref, v_ref, g_ref, b_ref, ws_ref, bst_ref, o_ref, *, n_chunks, n_groups):
    v = v_ref[...].astype(F32)
    mu = jnp.mean(v, axis=-1, keepdims=True)
    vc = v - mu
    var = jnp.mean(vc * vc, axis=-1, keepdims=True)
    vn = (vc * lax.rsqrt(var + LN_EPS) * g_ref[...] + b_ref[...]).astype(BF16)

    row = lax.broadcasted_iota(jnp.int32, (CHUNK, CHUNK), 0)
    col = lax.broadcasted_iota(jnp.int32, (CHUNK, CHUNK), 1)
    causal = row >= col
    for g in range(n_groups):
        cols = slice(g * GROUP_DIM, (g + 1) * GROUP_DIM)
        w = jnp.where(causal, ws_ref[g], 0.0).astype(BF16)
        bias = bst_ref[:, g:g + 1]
        for c in range(n_chunks):
            rows = slice(c * CHUNK, (c + 1) * CHUNK)
            s = jnp.dot(w, vn[rows, cols], preferred_element_type=F32) + bias
            o_ref[rows, cols] = (u_ref[rows, cols].astype(F32) * s).astype(o_ref.dtype)


def spatial_gating(z, ln_g, ln_b, w_s, b_s, *, tm=256):
    s, w2 = z.shape
    wdt = w2 // 2
    n_groups = wdt // GROUP_DIM
    tm = _tile(s, tm)
    assert tm % CHUNK == 0 and w_s.shape == (n_groups, CHUNK, CHUNK)
    return pl.pallas_call(
        functools.partial(_sgu_kernel, n_chunks=tm // CHUNK, n_groups=n_groups),
        grid=(s // tm,),
        in_specs=[pl.BlockSpec((tm, wdt), lambda i: (i, 0)),
                  pl.BlockSpec((tm, wdt), lambda i: (i, 1)),
                  pl.BlockSpec((1, wdt), lambda i: (0, 0)),
                  pl.BlockSpec((1, wdt), lambda i: (0, 0)),
                  pl.BlockSpec((n_groups, CHUNK, CHUNK), lambda i: (0, 0, 0)),
                  pl.BlockSpec((CHUNK, n_groups), lambda i: (0, 0))],
        out_specs=pl.BlockSpec((tm, wdt), lambda i: (i, 0)),
        out_shape=jax.ShapeDtypeStruct((s, wdt), BF16),
        compiler_params=_params(1),
        name="spatial_gating",
    )(z, z, ln_g.reshape(1, wdt), ln_b.reshape(1, wdt), w_s, b_s.T)


def _attn_kernel(q_ref, k_ref, v_ref, o_ref, acc_ref, run_ref, *, t, n_sub, scale):
    blk = pl.program_id(1)
    row = lax.broadcasted_iota(jnp.int32, (t, t), 0)
    col = lax.broadcasted_iota(jnp.int32, (t, t), 1)
    later = (row > col).astype(BF16)
    strictly_causal = col < row

    def visit(q, kt, run, keep):
        start = pl.multiple_of(kt * t, t)
        k = k_ref[pl.ds(start, t), :]
        v = v_ref[pl.ds(start, t), :]
        z = lax.dot_general(q, k, (((1,), (1,)), ((), ())), preferred_element_type=F32) * scale
        sp = jnp.maximum(z, 0.0) + jnp.log(1.0 + jnp.exp(-jnp.abs(z)))
        if keep is not None:
            sp = jnp.where(keep, sp, 0.0)
        hi = sp.astype(BF16)
        lo = (sp - hi.astype(F32)).astype(BF16)
        within = (jnp.dot(hi, later, preferred_element_type=F32)
                  + jnp.dot(lo, later, preferred_element_type=F32))
        a = jnp.exp(z - sp - within - run)
        if keep is not None:
            a = jnp.where(keep, a, 0.0)
        pv = jnp.dot(a.astype(BF16), v, preferred_element_type=F32)
        return pv, within[:, :1] + sp[:, :1]

    def q_rows(sub):
        return q_ref[sub * t:(sub + 1) * t, :]

    for sub in range(n_sub):
        kd = n_sub * blk + sub
        pv_d, tot_d = visit(q_rows(sub), kd, 0.0, strictly_causal)
        has_prev = blk > 0 if sub == 0 else None
        pv_p, tot_p = visit(q_rows(sub), jnp.maximum(kd - 1, 0), tot_d, has_prev)
        acc_ref[sub] = pv_d + pv_p
        run_ref[sub] = tot_d + tot_p

    def more(n):
        return jnp.logical_and(n_sub * blk + n_sub - 3 - n >= 0, jnp.min(run_ref[...]) < EXP_ZERO_F32)

    def step(n):
        for sub in range(n_sub):
            kt = n_sub * blk + sub - 2 - n
            keep = kt >= 0 if sub < n_sub - 1 else None
            pv, tot = visit(q_rows(sub), jnp.maximum(kt, 0), run_ref[sub], keep)
            acc_ref[sub] += pv
            run_ref[sub] += tot
        return n + 1

    lax.while_loop(more, step, 0)
    for sub in range(n_sub):
        o_ref[sub * t:(sub + 1) * t, :] = acc_ref[sub].astype(o_ref.dtype)


def stick_breaking_attention(qkv, *, t=256, n_sub=4):
    s, d3 = qkv.shape
    d = d3 // 3
    n_heads = d // HEAD_DIM
    assert s % (n_sub * t) == 0
    rows = n_sub * t
    return pl.pallas_call(
        functools.partial(_attn_kernel, t=t, n_sub=n_sub, scale=HEAD_DIM ** -0.5),
        grid=(n_heads, s // rows),
        in_specs=[pl.BlockSpec((rows, HEAD_DIM), lambda h, i: (i, h)),
                  pl.BlockSpec((s, HEAD_DIM), lambda h, i: (0, n_heads + h)),
                  pl.BlockSpec((s, HEAD_DIM), lambda h, i: (0, 2 * n_heads + h))],
        out_specs=pl.BlockSpec((rows, HEAD_DIM), lambda h, i: (i, h)),
        out_shape=jax.ShapeDtypeStruct((s, d), BF16),
        scratch_shapes=[pltpu.VMEM((n_sub, t, HEAD_DIM), F32),
                        pltpu.VMEM((n_sub, t, 1), F32)],
        compiler_params=_params(2),
        name="stick_breaking_attention",
    )(qkv, qkv, qkv)


def kernel(x, norm_mix, a_w_in, a_ln_g, a_ln_b, a_w_s, a_b_s, a_w_out, b_w_qkv, b_w_o,
           norm_ffn, ffn_w_gate_up, ffn_w_down, final_norm):
    b, s, d = x.shape
    assert b == 1, "kernels treat the sequence as one batch row"
    depth = norm_mix.shape[0]
    d_ff = ffn_w_down.shape[1]
    tk_down = -(-d_ff // (DOWN_K_TILES * FF_TILE)) * FF_TILE
    w_in, w_out, w_qkv, w_o, w_gate_up, w_down = (
        w.astype(BF16) for w in (a_w_in, a_w_out, b_w_qkv, b_w_o, ffn_w_gate_up, ffn_w_down))
    xs = x.reshape(s, d)
    xg, ssq = gain_sumsq(xs, norm_mix[0])
    for i in range(depth):
        j = i // 2
        if i % 2 == 0:
            z = normed_matmul(xg, ssq, w_in, j, act="gelu")
            y = spatial_gating(z, a_ln_g[j], a_ln_b[j], a_w_s[j], a_b_s[j])
            w_mix_out = w_out
        else:
            qkv = normed_matmul(xg, ssq, w_qkv, j)
            y = stick_breaking_attention(qkv)
            w_mix_out = w_o
        xs, xg, ssq = matmul_residual(y, w_mix_out, j, xs, norm_ffn[i], tn=512)
        hid = normed_swiglu(xg, ssq, w_gate_up, i)
        if i + 1 < depth:
            xs, xg, ssq = matmul_residual(hid, w_down, i, xs, norm_mix[i + 1], tk=tk_down)
        else:
            (xs,) = matmul_residual(hid, w_down, i, xs, final_norm, emit_norm=False, tk=tk_down)
    out = rmsnorm(xs, final_norm, F32)
    return out.reshape(b, s, d)
```

```python
import functools
import math

import jax
import jax.numpy as jnp
from jax import lax
from jax.experimental import pallas as pl
from jax.experimental.pallas import tpu as pltpu

NORM_EPS = 1e-6
LN_EPS = 1e-5
CHUNK = 128
GROUP_DIM = 128
HEAD_DIM = 256
FF_TILE = 256
DOWN_K_TILES = 4

EXP_ZERO_F32 = 104.0

V7X_VMEM_LIMIT_BYTES = 56 * 1024 * 1024

BF16 = jnp.bfloat16
F32 = jnp.float32


def _params(n_axes):
    return pltpu.CompilerParams(
        dimension_semantics=("arbitrary",) * n_axes,
        vmem_limit_bytes=V7X_VMEM_LIMIT_BYTES,
    )


def _tile(dim, pref):
    t = min(dim, pref)
    while dim % t:
        t -= 128
    return t


def _row_scale(ssq, width):
    return lax.rsqrt(ssq * (1.0 / width) + NORM_EPS)


def _gain_sumsq_kernel(x_ref, g_ref, xg_ref, ssq_ref):
    x = x_ref[...]
    xg_ref[...] = (x * g_ref[...]).astype(xg_ref.dtype)
    ssq_ref[...] = jnp.sum(x * x, axis=-1, keepdims=True)


def gain_sumsq(x, g):
    s, d = x.shape
    tm = _tile(s, 256)
    return pl.pallas_call(
        _gain_sumsq_kernel,
        grid=(s // tm,),
        in_specs=[pl.BlockSpec((tm, d), lambda i: (i, 0)),
                  pl.BlockSpec((1, d), lambda i: (0, 0))],
        out_specs=[pl.BlockSpec((tm, d), lambda i: (i, 0)),
                   pl.BlockSpec((tm, 1), lambda i: (i, 0))],
        out_shape=[jax.ShapeDtypeStruct((s, d), BF16),
                   jax.ShapeDtypeStruct((s, 1), F32)],
        compiler_params=_params(1),
        name="gain_sumsq",
    )(x, g.reshape(1, d))


def _rmsnorm_kernel(x_ref, g_ref, o_ref):
    x = x_ref[...]
    ms = jnp.mean(x * x, axis=-1, keepdims=True)
    o_ref[...] = (x * lax.rsqrt(ms + NORM_EPS) * g_ref[...]).astype(o_ref.dtype)


def rmsnorm(x, g, out_dtype):
    s, d = x.shape
    tm = _tile(s, 256)
    return pl.pallas_call(
        _rmsnorm_kernel,
        grid=(s // tm,),
        in_specs=[pl.BlockSpec((tm, d), lambda i: (i, 0)),
                  pl.BlockSpec((1, d), lambda i: (0, 0))],
        out_specs=pl.BlockSpec((tm, d), lambda i: (i, 0)),
        out_shape=jax.ShapeDtypeStruct((s, d), out_dtype),
        compiler_params=_params(1),
        name="rmsnorm",
    )(x, g.reshape(1, d))


def _gelu_exact(x):
    return 0.5 * x * (1.0 + lax.erf(x * (2.0 ** -0.5)))


def _normed_mm_kernel(a_ref, ssq_ref, w_ref, o_ref, *, act):
    acc = jnp.dot(a_ref[...], w_ref[...], preferred_element_type=F32)
    acc = acc * _row_scale(ssq_ref[...], a_ref.shape[1])
    if act == "gelu":
        acc = _gelu_exact(acc)
    o_ref[...] = acc.astype(o_ref.dtype)


def normed_matmul(xg, ssq, w_stack, layer, *, act=None, tm=1024, tn=1024):
    m, k = xg.shape
    _, _, n = w_stack.shape
    tm, tn = _tile(m, tm), _tile(n, tn)
    return pl.pallas_call(
        functools.partial(_normed_mm_kernel, act=act),
        grid=(m // tm, n // tn),
        in_specs=[pl.BlockSpec((tm, k), lambda i, j: (i, 0)),
                  pl.BlockSpec((tm, 1), lambda i, j: (i, 0)),
                  pl.BlockSpec((None, k, tn), lambda i, j: (layer, 0, j))],
        out_specs=pl.BlockSpec((tm, tn), lambda i, j: (i, j)),
        out_shape=jax.ShapeDtypeStruct((m, n), BF16),
        compiler_params=_params(2),
        name="normed_matmul_" + (act or "plain"),
    )(xg, ssq, w_stack)


def _normed_swiglu_kernel(a_ref, ssq_ref, *refs, tiles_per_step, n_tail):
    w_refs, o_ref = refs[:-1], refs[-1]

    def compute(n_tiles_here):
        a = a_ref[...]
        r = _row_scale(ssq_ref[...], a_ref.shape[1])
        for t in range(n_tiles_here):
            gate = jnp.dot(a, w_refs[t][...], preferred_element_type=F32) * r
            up = jnp.dot(a, w_refs[tiles_per_step + t][...], preferred_element_type=F32) * r
            o_ref[:, t * FF_TILE:(t + 1) * FF_TILE] = (gate * jax.nn.sigmoid(gate) * up).astype(o_ref.dtype)

    if n_tail == 0:
        compute(tiles_per_step)
        return
    is_last = pl.program_id(1) == pl.num_programs(1) - 1

    @pl.when(jnp.logical_not(is_last))
    def _():
        compute(tiles_per_step)

    @pl.when(is_last)
    def _():
        compute(n_tail)


def normed_swiglu(xg, ssq, w_stack, layer, *, tm=1024, tiles_per_step=3):
    m, k = xg.shape
    d_ff = w_stack.shape[2] // 2
    assert d_ff % FF_TILE == 0
    n_tiles = d_ff // FF_TILE
    tm = _tile(m, tm)

    def w_spec(first_tile, t):
        return pl.BlockSpec((None, k, FF_TILE),
                            lambda i, j: (layer, 0, first_tile + jnp.minimum(tiles_per_step * j + t, n_tiles - 1)))

    w_specs = [w_spec(first, t) for first in (0, n_tiles) for t in range(tiles_per_step)]
    return pl.pallas_call(
        functools.partial(_normed_swiglu_kernel, tiles_per_step=tiles_per_step, n_tail=n_tiles % tiles_per_step),
        grid=(m // tm, pl.cdiv(n_tiles, tiles_per_step)),
        in_specs=[pl.BlockSpec((tm, k), lambda i, j: (i, 0)),
                  pl.BlockSpec((tm, 1), lambda i, j: (i, 0))] + w_specs,
        out_specs=pl.BlockSpec((tm, tiles_per_step * FF_TILE), lambda i, j: (i, j)),
        out_shape=jax.ShapeDtypeStruct((m, d_ff), BF16),
        compiler_params=_params(2),
        name="normed_swiglu",
    )(xg, ssq, *([w_stack] * (2 * tiles_per_step)))


def _mm_res_kernel(a_ref, w_ref, r_ref, g_ref, o_ref, *norm_refs, nk, k_last):
    j, kk = pl.program_id(1), pl.program_id(2)

    def partial_product(k_used=None):
        return jnp.dot(a_ref[:, :k_used], w_ref[:k_used, :], preferred_element_type=F32)

    def emit_norm(xn):
        xg_ref, ssq_ref = norm_refs
        xg_ref[...] = (xn * g_ref[...]).astype(xg_ref.dtype)
        s = jnp.sum(xn * xn, axis=-1, keepdims=True)

        @pl.when(j == 0)
        def _():
            ssq_ref[...] = s

        @pl.when(j != 0)
        def _():
            ssq_ref[...] += s

    if nk == 1:
        xn = r_ref[...] + partial_product(k_last)
        o_ref[...] = xn
        if norm_refs:
            emit_norm(xn)
        return

    @pl.when(kk == 0)
    def _():
        o_ref[...] = r_ref[...] + partial_product()

    @pl.when(jnp.logical_and(kk != 0, kk != nk - 1))
    def _():
        o_ref[...] += partial_product()

    @pl.when(kk == nk - 1)
    def _():
        xn = o_ref[...] + partial_product(k_last)
        o_ref[...] = xn
        if norm_refs:
            emit_norm(xn)


def matmul_residual(a, w_stack, layer, res, g_next, *, emit_norm=True, tm=1024, tn=1024, tk=4096):
    m, k = a.shape
    _, _, n = w_stack.shape
    tm, tn, tk = _tile(m, tm), _tile(n, tn), min(tk, k)
    nk = pl.cdiv(k, tk)
    k_last = k - (nk - 1) * tk
    assert tk % 128 == 0 and k_last % 128 == 0
    out_specs = [pl.BlockSpec((tm, tn), lambda i, j, kk: (i, j))]
    out_shape = [jax.ShapeDtypeStruct((m, n), F32)]
    if emit_norm:
        out_specs += [pl.BlockSpec((tm, tn), lambda i, j, kk: (i, j)),
                      pl.BlockSpec((tm, 1), lambda i, j, kk: (i, 0))]
        out_shape += [jax.ShapeDtypeStruct((m, n), BF16), jax.ShapeDtypeStruct((m, 1), F32)]
    return pl.pallas_call(
        functools.partial(_mm_res_kernel, nk=nk, k_last=k_last),
        grid=(m // tm, n // tn, nk),
        in_specs=[pl.BlockSpec((tm, tk), lambda i, j, kk: (i, kk)),
                  pl.BlockSpec((None, tk, tn), lambda i, j, kk: (layer, kk, j)),
                  pl.BlockSpec((tm, tn), lambda i, j, kk: (i, j)),
                  pl.BlockSpec((1, tn), lambda i, j, kk: (0, j))],
        out_specs=out_specs,
        out_shape=out_shape,
        compiler_params=_params(3),
        name="matmul_residual",
    )(a, w_stack, res, g_next.reshape(1, n))


def _sgu_kernel(u_ref, v_ref, g_ref, b_ref, ws_ref, bst_ref, o_ref, *, n_chunks, n_groups):
    v = v_ref[...].astype(F32)
    mu = jnp.mean(v, axis=-1, keepdims=True)
    vc = v - mu
    var = jnp.mean(vc * vc, axis=-1, keepdims=True)
    vn = (vc * lax.rsqrt(var + LN_EPS) * g_ref[...] + b_ref[...]).astype(BF16)

    row = lax.broadcasted_iota(jnp.int32, (CHUNK, CHUNK), 0)
    col = lax.broadcasted_iota(jnp.int32, (CHUNK, CHUNK), 1)
    causal = row >= col
    for g in range(n_groups):
        cols = slice(g * GROUP_DIM, (g + 1) * GROUP_DIM)
        w = jnp.where(causal, ws_ref[g], 0.0).astype(BF16)
        bias = bst_ref[:, g:g + 1]
        for c in range(n_chunks):
            rows = slice(c * CHUNK, (c + 1) * CHUNK)
            s = jnp.dot(w, vn[rows, cols], preferred_element_type=F32) + bias
            o_ref[rows, cols] = (u_ref[rows, cols].astype(F32) * s).astype(o_ref.dtype)


def spatial_gating(z, ln_g, ln_b, w_s, b_s, *, tm=256):
    s, w2 = z.shape
    wdt = w2 // 2
    n_groups = wdt // GROUP_DIM
    tm = _tile(s, tm)
    assert tm % CHUNK == 0 and w_s.shape == (n_groups, CHUNK, CHUNK)
    return pl.pallas_call(
        functools.partial(_sgu_kernel, n_chunks=tm // CHUNK, n_groups=n_groups),
        grid=(s // tm,),
        in_specs=[pl.BlockSpec((tm, wdt), lambda i: (i, 0)),
                  pl.BlockSpec((tm, wdt), lambda i: (i, 1)),
                  pl.BlockSpec((1, wdt), lambda i: (0, 0)),
                  pl.BlockSpec((1, wdt), lambda i: (0, 0)),
                  pl.BlockSpec((n_groups, CHUNK, CHUNK), lambda i: (0, 0, 0)),
                  pl.BlockSpec((CHUNK, n_groups), lambda i: (0, 0))],
        out_specs=pl.BlockSpec((tm, wdt), lambda i: (i, 0)),
        out_shape=jax.ShapeDtypeStruct((s, wdt), BF16),
        compiler_params=_params(1),
        name="spatial_gating",
    )(z, z, ln_g.reshape(1, wdt), ln_b.reshape(1, wdt), w_s, b_s.T)


def _attn_kernel(q_ref, k_ref, v_ref, o_ref, acc_ref, run_ref, *, t, n_sub, scale):
    blk = pl.program_id(1)
    row = lax.broadcasted_iota(jnp.int32, (t, t), 0)
    col = lax.broadcasted_iota(jnp.int32, (t, t), 1)
    later = (row > col).astype(BF16)
    strictly_causal = col < row

    def visit(q, kt, run, keep):
        start = pl.multiple_of(kt * t, t)
        k = k_ref[pl.ds(start, t), :]
        v = v_ref[pl.ds(start, t), :]
        z = lax.dot_general(q, k, (((1,), (1,)), ((), ())), preferred_element_type=F32)
        sp = jnp.maximum(z, 0.0) + jnp.log(1.0 + jnp.exp(-jnp.abs(z)))
        if keep is not None:
            sp = jnp.where(keep, sp, 0.0)
        within = jnp.dot(sp.astype(BF16), later, preferred_element_type=F32)
        a = jnp.exp(z - sp - within - run)
        if keep is not None:
            a = jnp.where(keep, a, 0.0)
        pv = jnp.dot(a.astype(BF16), v, preferred_element_type=F32)
        return pv, within[:, :1] + sp[:, :1]

    def q_rows(sub):
        return q_ref[sub * t:(sub + 1) * t, :] * jnp.asarray(scale, BF16)

    for sub in range(n_sub):
        kd = n_sub * blk + sub
        pv_d, tot_d = visit(q_rows(sub), kd, 0.0, strictly_causal)
        has_prev = blk > 0 if sub == 0 else None
        pv_p, tot_p = visit(q_rows(sub), jnp.maximum(kd - 1, 0), tot_d, has_prev)
        acc_ref[sub] = pv_d + pv_p
        run_ref[sub] = tot_d + tot_p

    def more(n):
        needed = False
        for sub in range(n_sub):
            has_tiles_left = n_sub * blk + sub - 2 - n >= 0
            needed = jnp.logical_or(needed, jnp.logical_and(has_tiles_left, jnp.min(run_ref[sub]) < EXP_ZERO_F32))
        return needed

    def step(n):
        for sub in range(n_sub):
            kt = n_sub * blk + sub - 2 - n
            keep = kt >= 0 if sub < n_sub - 1 else None
            pv, tot = visit(q_rows(sub), jnp.maximum(kt, 0), run_ref[sub], keep)
            acc_ref[sub] += pv
            run_ref[sub] += tot
        return n + 1

    lax.while_loop(more, step, 0)
    for sub in range(n_sub):
        o_ref[sub * t:(sub + 1) * t, :] = acc_ref[sub].astype(o_ref.dtype)


def stick_breaking_attention(qkv, *, t=256, n_sub=4):
    s, d3 = qkv.shape
    d = d3 // 3
    n_heads = d // HEAD_DIM
    assert s % (n_sub * t) == 0
    assert HEAD_DIM == 4 ** round(math.log(HEAD_DIM, 4)), "1/sqrt(HEAD_DIM) must be a power of two"
    rows = n_sub * t
    return pl.pallas_call(
        functools.partial(_attn_kernel, t=t, n_sub=n_sub, scale=HEAD_DIM ** -0.5),
        grid=(n_heads, s // rows),
        in_specs=[pl.BlockSpec((rows, HEAD_DIM), lambda h, i: (i, h)),
                  pl.BlockSpec((s, HEAD_DIM), lambda h, i: (0, n_heads + h)),
                  pl.BlockSpec((s, HEAD_DIM), lambda h, i: (0, 2 * n_heads + h))],
        out_specs=pl.BlockSpec((rows, HEAD_DIM), lambda h, i: (i, h)),
        out_shape=jax.ShapeDtypeStruct((s, d), BF16),
        scratch_shapes=[pltpu.VMEM((n_sub, t, HEAD_DIM), F32),
                        pltpu.VMEM((n_sub, t, 1), F32)],
        compiler_params=_params(2),
        name="stick_breaking_attention",
    )(qkv, qkv, qkv)


def kernel(x, norm_mix, a_w_in, a_ln_g, a_ln_b, a_w_s, a_b_s, a_w_out, b_w_qkv, b_w_o,
           norm_ffn, ffn_w_gate_up, ffn_w_down, final_norm):
    b, s, d = x.shape
    assert b == 1, "kernels treat the sequence as one batch row"
    depth = norm_mix.shape[0]
    d_ff = ffn_w_down.shape[1]
    tk_down = -(-d_ff // (DOWN_K_TILES * FF_TILE)) * FF_TILE
    w_in, w_out, w_qkv, w_o, w_gate_up, w_down = (
        w.astype(BF16) for w in (a_w_in, a_w_out, b_w_qkv, b_w_o, ffn_w_gate_up, ffn_w_down))
    xs = x.reshape(s, d)
    xg, ssq = gain_sumsq(xs, norm_mix[0])
    for i in range(depth):
        j = i // 2
        if i % 2 == 0:
            z = normed_matmul(xg, ssq, w_in, j, act="gelu")
            y = spatial_gating(z, a_ln_g[j], a_ln_b[j], a_w_s[j], a_b_s[j])
            w_mix_out = w_out
        else:
            qkv = normed_matmul(xg, ssq, w_qkv, j)
            y = stick_breaking_attention(qkv)
            w_mix_out = w_o
        xs, xg, ssq = matmul_residual(y, w_mix_out, j, xs, norm_ffn[i], tn=512)
        hid = normed_swiglu(xg, ssq, w_gate_up, i)
        if i + 1 < depth:
            xs, xg, ssq = matmul_residual(hid, w_down, i, xs, norm_mix[i + 1], tk=tk_down)
        else:
            (xs,) = matmul_residual(hid, w_down, i, xs, final_norm, emit_norm=False, tk=tk_down)
    out = rmsnorm(xs, final_norm, F32)
    return out.reshape(b, s, d)
```

```python
import functools
import math

import jax
import jax.numpy as jnp
from jax import lax
from jax.experimental import pallas as pl
from jax.experimental.pallas import tpu as pltpu

NORM_EPS = 1e-6
LN_EPS = 1e-5
CHUNK = 128
GROUP_DIM = 128
HEAD_DIM = 256
FF_TILE = 256
DOWN_K_TILES = 4

EXP_ZERO_F32 = 104.0

V7X_VMEM_LIMIT_BYTES = 60 * 1024 * 1024

BF16 = jnp.bfloat16
F32 = jnp.float32


def _params(n_axes):
    return pltpu.CompilerParams(
        dimension_semantics=("arbitrary",) * n_axes,
        vmem_limit_bytes=V7X_VMEM_LIMIT_BYTES,
    )


def _tile(dim, pref):
    t = min(dim, pref)
    while dim % t:
        t -= 128
    return t


def _row_scale(ssq, width):
    return lax.rsqrt(ssq * (1.0 / width) + NORM_EPS)


def _gain_sumsq_kernel(x_ref, g_ref, xg_ref, ssq_ref):
    x = x_ref[...]
    xg_ref[...] = (x * g_ref[...]).astype(xg_ref.dtype)
    ssq_ref[...] = jnp.sum(x * x, axis=-1, keepdims=True)


def gain_sumsq(x, g):
    s, d = x.shape
    tm = _tile(s, 256)
    return pl.pallas_call(
        _gain_sumsq_kernel,
        grid=(s // tm,),
        in_specs=[pl.BlockSpec((tm, d), lambda i: (i, 0)),
                  pl.BlockSpec((1, d), lambda i: (0, 0))],
        out_specs=[pl.BlockSpec((tm, d), lambda i: (i, 0)),
                   pl.BlockSpec((tm, 1), lambda i: (i, 0))],
        out_shape=[jax.ShapeDtypeStruct((s, d), BF16),
                   jax.ShapeDtypeStruct((s, 1), F32)],
        compiler_params=_params(1),
        name="gain_sumsq",
    )(x, g.reshape(1, d))


def _rmsnorm_kernel(x_ref, g_ref, o_ref):
    x = x_ref[...]
    ms = jnp.mean(x * x, axis=-1, keepdims=True)
    o_ref[...] = (x * lax.rsqrt(ms + NORM_EPS) * g_ref[...]).astype(o_ref.dtype)


def rmsnorm(x, g, out_dtype):
    s, d = x.shape
    tm = _tile(s, 256)
    return pl.pallas_call(
        _rmsnorm_kernel,
        grid=(s // tm,),
        in_specs=[pl.BlockSpec((tm, d), lambda i: (i, 0)),
                  pl.BlockSpec((1, d), lambda i: (0, 0))],
        out_specs=pl.BlockSpec((tm, d), lambda i: (i, 0)),
        out_shape=jax.ShapeDtypeStruct((s, d), out_dtype),
        compiler_params=_params(1),
        name="rmsnorm",
    )(x, g.reshape(1, d))


def _attach_ride_along_cast(cast_next, n_row_steps, n_col_steps, steps_of, in_specs, operands, out_specs, out_shape):
    src_stack, layer = cast_next
    _, r, c = src_stack.shape
    assert r % n_row_steps == 0 and (r // n_row_steps) % 16 == 0
    rb = r // n_row_steps
    cb = -(-pl.cdiv(c, n_col_steps) // 128) * 128
    last = pl.cdiv(c, cb) - 1

    def block(*idx):
        row_step, col_step = steps_of(*idx)
        return row_step, jnp.minimum(col_step, last)

    in_specs.append(pl.BlockSpec((None, rb, cb), lambda *idx: (layer,) + block(*idx)))
    operands.append(src_stack)
    out_specs.append(pl.BlockSpec((rb, cb), block))
    out_shape.append(jax.ShapeDtypeStruct((r, c), BF16))


def _gelu_exact(x):
    return 0.5 * x * (1.0 + lax.erf(x * (2.0 ** -0.5)))


def _normed_mm_kernel(a_ref, ssq_ref, w_ref, *refs, act, ride_along):
    if ride_along:
        cast_src_ref, o_ref, cast_dst_ref = refs
        cast_dst_ref[...] = cast_src_ref[...].astype(cast_dst_ref.dtype)
    else:
        (o_ref,) = refs
    acc = jnp.dot(a_ref[...], w_ref[...], preferred_element_type=F32)
    acc = acc * _row_scale(ssq_ref[...], a_ref.shape[1])
    if act == "gelu":
        acc = _gelu_exact(acc)
    o_ref[...] = acc.astype(o_ref.dtype)


def normed_matmul(xg, ssq, w, *, act=None, cast_next=None, tm=1024, tn=1024):
    m, k = xg.shape
    _, n = w.shape
    tm, tn = _tile(m, tm), _tile(n, tn)
    grid = (m // tm, n // tn)
    in_specs = [pl.BlockSpec((tm, k), lambda i, j: (i, 0)),
                pl.BlockSpec((tm, 1), lambda i, j: (i, 0)),
                pl.BlockSpec((k, tn), lambda i, j: (0, j))]
    operands = [xg, ssq, w]
    out_specs = [pl.BlockSpec((tm, tn), lambda i, j: (i, j))]
    out_shape = [jax.ShapeDtypeStruct((m, n), BF16)]
    if cast_next is not None:
        _attach_ride_along_cast(cast_next, *grid, lambda i, j: (i, j), in_specs, operands, out_specs, out_shape)
    outs = pl.pallas_call(
        functools.partial(_normed_mm_kernel, act=act, ride_along=cast_next is not None),
        grid=grid,
        in_specs=in_specs,
        out_specs=out_specs,
        out_shape=out_shape,
        compiler_params=_params(2),
        name="normed_matmul_" + (act or "plain"),
    )(*operands)
    return outs if cast_next is not None else (outs[0], None)


def _normed_swiglu_kernel(a_ref, ssq_ref, *refs, tiles_per_step, n_tail, ride_along):
    w_refs, refs = refs[:2 * tiles_per_step], refs[2 * tiles_per_step:]
    if ride_along:
        cast_src_ref, o_ref, cast_dst_ref = refs
    else:
        (o_ref,) = refs

    def compute(n_tiles_here):
        if ride_along:
            cast_dst_ref[...] = cast_src_ref[...].astype(cast_dst_ref.dtype)
        a = a_ref[...]
        r = _row_scale(ssq_ref[...], a_ref.shape[1])
        for t in range(n_tiles_here):
            gate = jnp.dot(a, w_refs[t][...], preferred_element_type=F32) * r
            up = jnp.dot(a, w_refs[tiles_per_step + t][...], preferred_element_type=F32) * r
            o_ref[:, t * FF_TILE:(t + 1) * FF_TILE] = (gate * jax.nn.sigmoid(gate) * up).astype(o_ref.dtype)

    if n_tail == 0:
        compute(tiles_per_step)
        return
    is_last = pl.program_id(1) == pl.num_programs(1) - 1

    @pl.when(jnp.logical_not(is_last))
    def _():
        compute(tiles_per_step)

    @pl.when(is_last)
    def _():
        compute(n_tail)


def normed_swiglu(xg, ssq, w, *, cast_next=None, tm=1024, tiles_per_step=3):
    m, k = xg.shape
    d_ff = w.shape[1] // 2
    assert d_ff % FF_TILE == 0
    n_tiles = d_ff // FF_TILE
    tm = _tile(m, tm)
    grid = (m // tm, pl.cdiv(n_tiles, tiles_per_step))

    def w_spec(first_tile, t):
        return pl.BlockSpec((k, FF_TILE),
                            lambda i, j: (0, first_tile + jnp.minimum(tiles_per_step * j + t, n_tiles - 1)))

    in_specs = [pl.BlockSpec((tm, k), lambda i, j: (i, 0)),
                pl.BlockSpec((tm, 1), lambda i, j: (i, 0))]
    in_specs += [w_spec(first, t) for first in (0, n_tiles) for t in range(tiles_per_step)]
    operands = [xg, ssq] + [w] * (2 * tiles_per_step)
    out_specs = [pl.BlockSpec((tm, tiles_per_step * FF_TILE), lambda i, j: (i, j))]
    out_shape = [jax.ShapeDtypeStruct((m, d_ff), BF16)]
    if cast_next is not None:
        _attach_ride_along_cast(cast_next, *grid, lambda i, j: (i, j), in_specs, operands, out_specs, out_shape)
    outs = pl.pallas_call(
        functools.partial(_normed_swiglu_kernel, tiles_per_step=tiles_per_step, n_tail=n_tiles % tiles_per_step,
                          ride_along=cast_next is not None),
        grid=grid,
        in_specs=in_specs,
        out_specs=out_specs,
        out_shape=out_shape,
        compiler_params=_params(2),
        name="normed_swiglu",
    )(*operands)
    return outs if cast_next is not None else (outs[0], None)


def _mm_res_kernel(a_ref, w_ref, r_ref, g_ref, *refs, nk, k_last, emit_norm, ride_along):
    j, kk = pl.program_id(1), pl.program_id(2)
    if ride_along:
        cast_src_ref, *refs, cast_dst_ref = refs
    o_ref, *norm_refs = refs
    assert len(norm_refs) == (2 if emit_norm else 0)

    def partial_product(k_used=None):
        if ride_along:
            cast_dst_ref[...] = cast_src_ref[...].astype(cast_dst_ref.dtype)
        return jnp.dot(a_ref[:, :k_used], w_ref[:k_used, :], preferred_element_type=F32)

    def emit_norm(xn):
        xg_ref, ssq_ref = norm_refs
        xg_ref[...] = (xn * g_ref[...]).astype(xg_ref.dtype)
        s = jnp.sum(xn * xn, axis=-1, keepdims=True)

        @pl.when(j == 0)
        def _():
            ssq_ref[...] = s

        @pl.when(j != 0)
        def _():
            ssq_ref[...] += s

    if nk == 1:
        xn = r_ref[...] + partial_product(k_last)
        o_ref[...] = xn
        if norm_refs:
            emit_norm(xn)
        return

    @pl.when(kk == 0)
    def _():
        o_ref[...] = r_ref[...] + partial_product()

    @pl.when(jnp.logical_and(kk != 0, kk != nk - 1))
    def _():
        o_ref[...] += partial_product()

    @pl.when(kk == nk - 1)
    def _():
        xn = o_ref[...] + partial_product(k_last)
        o_ref[...] = xn
        if norm_refs:
            emit_norm(xn)


def matmul_residual(a, w, res, g_next, *, emit_norm=True, cast_next=None, tm=1024, tn=1024, tk=4096):
    m, k = a.shape
    _, n = w.shape
    tm, tn, tk = _tile(m, tm), _tile(n, tn), min(tk, k)
    nk = pl.cdiv(k, tk)
    k_last = k - (nk - 1) * tk
    assert tk % 128 == 0 and k_last % 128 == 0
    in_specs = [pl.BlockSpec((tm, tk), lambda i, j, kk: (i, kk)),
                pl.BlockSpec((tk, tn), lambda i, j, kk: (kk, j)),
                pl.BlockSpec((tm, tn), lambda i, j, kk: (i, j)),
                pl.BlockSpec((1, tn), lambda i, j, kk: (0, j))]
    operands = [a, w, res, g_next.reshape(1, n)]
    out_specs = [pl.BlockSpec((tm, tn), lambda i, j, kk: (i, j))]
    out_shape = [jax.ShapeDtypeStruct((m, n), F32)]
    if emit_norm:
        out_specs += [pl.BlockSpec((tm, tn), lambda i, j, kk: (i, j)),
                      pl.BlockSpec((tm, 1), lambda i, j, kk: (i, 0))]
        out_shape += [jax.ShapeDtypeStruct((m, n), BF16), jax.ShapeDtypeStruct((m, 1), F32)]
    if cast_next is not None:
        _attach_ride_along_cast(cast_next, m // tm, (n // tn) * nk, lambda i, j, kk: (i, j * nk + kk),
                                in_specs, operands, out_specs, out_shape)
    outs = list(pl.pallas_call(
        functools.partial(_mm_res_kernel, nk=nk, k_last=k_last, emit_norm=emit_norm,
                          ride_along=cast_next is not None),
        grid=(m // tm, n // tn, nk),
        in_specs=in_specs,
        out_specs=out_specs,
        out_shape=out_shape,
        compiler_params=_params(3),
        name="matmul_residual",
    )(*operands))
    cast = outs.pop() if cast_next is not None else None
    x_new, xg, ssq = outs if emit_norm else (outs[0], None, None)
    return x_new, xg, ssq, cast


def _sgu_kernel(u_ref, v_ref, g_ref, b_ref, ws_ref, bst_ref, o_ref, *, n_chunks, n_groups):
    v = v_ref[...].astype(F32)
    mu = jnp.mean(v, axis=-1, keepdims=True)
    vc = v - mu
    var = jnp.mean(vc * vc, axis=-1, keepdims=True)
    vn = (vc * lax.rsqrt(var + LN_EPS) * g_ref[...] + b_ref[...]).astype(BF16)

    row = lax.broadcasted_iota(jnp.int32, (CHUNK, CHUNK), 0)
    col = lax.broadcasted_iota(jnp.int32, (CHUNK, CHUNK), 1)
    causal = row >= col
    for g in range(n_groups):
        cols = slice(g * GROUP_DIM, (g + 1) * GROUP_DIM)
        w = jnp.where(causal, ws_ref[g], 0.0).astype(BF16)
        bias = bst_ref[:, g:g + 1]
        for c in range(n_chunks):
            rows = slice(c * CHUNK, (c + 1) * CHUNK)
            s = jnp.dot(w, vn[rows, cols], preferred_element_type=F32) + bias
            o_ref[rows, cols] = (u_ref[rows, cols].astype(F32) * s).astype(o_ref.dtype)


def spatial_gating(z, ln_g, ln_b, w_s, b_s, *, tm=256):
    s, w2 = z.shape
    wdt = w2 // 2
    n_groups = wdt // GROUP_DIM
    tm = _tile(s, tm)
    assert tm % CHUNK == 0 and w_s.shape == (n_groups, CHUNK, CHUNK)
    return pl.pallas_call(
        functools.partial(_sgu_kernel, n_chunks=tm // CHUNK, n_groups=n_groups),
        grid=(s // tm,),
        in_specs=[pl.BlockSpec((tm, wdt), lambda i: (i, 0)),
                  pl.BlockSpec((tm, wdt), lambda i: (i, 1)),
                  pl.BlockSpec((1, wdt), lambda i: (0, 0)),
                  pl.BlockSpec((1, wdt), lambda i: (0, 0)),
                  pl.BlockSpec((n_groups, CHUNK, CHUNK), lambda i: (0, 0, 0)),
                  pl.BlockSpec((CHUNK, n_groups), lambda i: (0, 0))],
        out_specs=pl.BlockSpec((tm, wdt), lambda i: (i, 0)),
        out_shape=jax.ShapeDtypeStruct((s, wdt), BF16),
        compiler_params=_params(1),
        name="spatial_gating",
    )(z, z, ln_g.reshape(1, wdt), ln_b.reshape(1, wdt), w_s, b_s.T)


def _attn_kernel(q_ref, k_ref, v_ref, o_ref, acc_ref, run_ref, *, t, n_sub, scale):
    blk = pl.program_id(1)
    row = lax.broadcasted_iota(jnp.int32, (t, t), 0)
    col = lax.broadcasted_iota(jnp.int32, (t, t), 1)
    later = (row > col).astype(BF16)
    strictly_causal = col < row

    def visit(q, kt, run, keep):
        start = pl.multiple_of(kt * t, t)
        k = k_ref[pl.ds(start, t), :]
        v = v_ref[pl.ds(start, t), :]
        z = lax.dot_general(q, k, (((1,), (1,)), ((), ())), preferred_element_type=F32)
        sp = jnp.maximum(z, 0.0) + jnp.log(1.0 + jnp.exp(-jnp.abs(z)))
        if keep is not None:
            sp = jnp.where(keep, sp, 0.0)
        within = jnp.dot(sp.astype(BF16), later, preferred_element_type=F32)
        a = jnp.exp(z - sp - within - run)
        if keep is not None:
            a = jnp.where(keep, a, 0.0)
        pv = jnp.dot(a.astype(BF16), v, preferred_element_type=F32)
        return pv, within[:, :1] + sp[:, :1]

    def q_rows(sub):
        return q_ref[sub * t:(sub + 1) * t, :] * jnp.asarray(scale, BF16)

    for sub in range(n_sub):
        kd = n_sub * blk + sub
        pv_d, tot_d = visit(q_rows(sub), kd, 0.0, strictly_causal)
        has_prev = blk > 0 if sub == 0 else None
        pv_p, tot_p = visit(q_rows(sub), jnp.maximum(kd - 1, 0), tot_d, has_prev)
        acc_ref[sub] = pv_d + pv_p
        run_ref[sub] = tot_d + tot_p

    def more(n):
        needed = False
        for sub in range(n_sub):
            has_tiles_left = n_sub * blk + sub - 2 - n >= 0
            needed = jnp.logical_or(needed, jnp.logical_and(has_tiles_left, jnp.min(run_ref[sub]) < EXP_ZERO_F32))
        return needed

    def step(n):
        for sub in range(n_sub):
            kt = n_sub * blk + sub - 2 - n
            keep = kt >= 0 if sub < n_sub - 1 else None
            pv, tot = visit(q_rows(sub), jnp.maximum(kt, 0), run_ref[sub], keep)
            acc_ref[sub] += pv
            run_ref[sub] += tot
        return n + 1

    lax.while_loop(more, step, 0)
    for sub in range(n_sub):
        o_ref[sub * t:(sub + 1) * t, :] = acc_ref[sub].astype(o_ref.dtype)


def stick_breaking_attention(qkv, *, t=256, n_sub=4):
    s, d3 = qkv.shape
    d = d3 // 3
    n_heads = d // HEAD_DIM
    assert s % (n_sub * t) == 0
    assert HEAD_DIM == 4 ** round(math.log(HEAD_DIM, 4)), "1/sqrt(HEAD_DIM) must be a power of two"
    rows = n_sub * t
    return pl.pallas_call(
        functools.partial(_attn_kernel, t=t, n_sub=n_sub, scale=HEAD_DIM ** -0.5),
        grid=(n_heads, s // rows),
        in_specs=[pl.BlockSpec((rows, HEAD_DIM), lambda h, i: (i, h)),
                  pl.BlockSpec((s, HEAD_DIM), lambda h, i: (0, n_heads + h)),
                  pl.BlockSpec((s, HEAD_DIM), lambda h, i: (0, 2 * n_heads + h))],
        out_specs=pl.BlockSpec((rows, HEAD_DIM), lambda h, i: (i, h)),
        out_shape=jax.ShapeDtypeStruct((s, d), BF16),
        scratch_shapes=[pltpu.VMEM((n_sub, t, HEAD_DIM), F32),
                        pltpu.VMEM((n_sub, t, 1), F32)],
        compiler_params=_params(2),
        name="stick_breaking_attention",
    )(qkv, qkv, qkv)


def kernel(x, norm_mix, a_w_in, a_ln_g, a_ln_b, a_w_s, a_b_s, a_w_out, b_w_qkv, b_w_o,
           norm_ffn, ffn_w_gate_up, ffn_w_down, final_norm):
    b, s, d = x.shape
    assert b == 1, "kernels treat the sequence as one batch row"
    depth = norm_mix.shape[0]
    d_ff = ffn_w_down.shape[1]
    tk_down = -(-d_ff // (DOWN_K_TILES * FF_TILE)) * FF_TILE
    def mixer_stacks(i):
        return (a_w_in, a_w_out) if i % 2 == 0 else (b_w_qkv, b_w_o)

    w_mix_in, w_mix_out, w_gate_up, w_down = (
        stack[0].astype(BF16) for stack in mixer_stacks(0) + (ffn_w_gate_up, ffn_w_down))
    xs = x.reshape(s, d)
    xg, ssq = gain_sumsq(xs, norm_mix[0])
    for i in range(depth):
        j = i // 2
        last = i + 1 == depth
        nxt_in, nxt_out, nxt_gate_up, nxt_down = (
            (None,) * 4 if last else
            tuple((stack, (i + 1) // 2) for stack in mixer_stacks(i + 1))
            + ((ffn_w_gate_up, i + 1), (ffn_w_down, i + 1)))
        if i % 2 == 0:
            z, w_mix_in = normed_matmul(xg, ssq, w_mix_in, act="gelu", cast_next=nxt_in)
            y = spatial_gating(z, a_ln_g[j], a_ln_b[j], a_w_s[j], a_b_s[j])
        else:
            qkv, w_mix_in = normed_matmul(xg, ssq, w_mix_in, cast_next=nxt_in)
            y = stick_breaking_attention(qkv)
        xs, xg, ssq, w_mix_out = matmul_residual(y, w_mix_out, xs, norm_ffn[i], cast_next=nxt_out, tn=512)
        hid, w_gate_up = normed_swiglu(xg, ssq, w_gate_up, cast_next=nxt_gate_up)
        xs, xg, ssq, w_down = matmul_residual(hid, w_down, xs, final_norm if last else norm_mix[i + 1],
                                              emit_norm=not last, cast_next=nxt_down, tk=tk_down)
    out = rmsnorm(xs, final_norm, F32)
    return out.reshape(b, s, d)
```

```python
import functools
import math

import jax
import jax.numpy as jnp
from jax import lax
from jax.experimental import pallas as pl
from jax.experimental.pallas import tpu as pltpu

NORM_EPS = 1e-6
LN_EPS = 1e-5
CHUNK = 128
GROUP_DIM = 128
HEAD_DIM = 256
FF_TILE = 256
DOWN_K_TILES = 4

EXP_ZERO_F32 = 104.0

V7X_VMEM_LIMIT_BYTES = 60 * 1024 * 1024

BF16 = jnp.bfloat16
F32 = jnp.float32


def _params(n_axes):
    return pltpu.CompilerParams(
        dimension_semantics=("arbitrary",) * n_axes,
        vmem_limit_bytes=V7X_VMEM_LIMIT_BYTES,
    )


def _tile(dim, pref):
    t = min(dim, pref)
    while dim % t:
        t -= 128
    return t


def _row_scale(ssq, width):
    return lax.rsqrt(ssq * (1.0 / width) + NORM_EPS)


def _gain_sumsq_kernel(x_ref, g_ref, xg_ref, ssq_ref):
    x = x_ref[...]
    xg_ref[...] = (x * g_ref[...]).astype(xg_ref.dtype)
    ssq_ref[...] = jnp.sum(x * x, axis=-1, keepdims=True)


def gain_sumsq(x, g):
    s, d = x.shape
    tm = _tile(s, 256)
    return pl.pallas_call(
        _gain_sumsq_kernel,
        grid=(s // tm,),
        in_specs=[pl.BlockSpec((tm, d), lambda i: (i, 0)),
                  pl.BlockSpec((1, d), lambda i: (0, 0))],
        out_specs=[pl.BlockSpec((tm, d), lambda i: (i, 0)),
                   pl.BlockSpec((tm, 1), lambda i: (i, 0))],
        out_shape=[jax.ShapeDtypeStruct((s, d), BF16),
                   jax.ShapeDtypeStruct((s, 1), F32)],
        compiler_params=_params(1),
        name="gain_sumsq",
    )(x, g.reshape(1, d))


def _rmsnorm_kernel(x_ref, g_ref, o_ref):
    x = x_ref[...]
    ms = jnp.mean(x * x, axis=-1, keepdims=True)
    o_ref[...] = (x * lax.rsqrt(ms + NORM_EPS) * g_ref[...]).astype(o_ref.dtype)


def rmsnorm(x, g, out_dtype):
    s, d = x.shape
    tm = _tile(s, 256)
    return pl.pallas_call(
        _rmsnorm_kernel,
        grid=(s // tm,),
        in_specs=[pl.BlockSpec((tm, d), lambda i: (i, 0)),
                  pl.BlockSpec((1, d), lambda i: (0, 0))],
        out_specs=pl.BlockSpec((tm, d), lambda i: (i, 0)),
        out_shape=jax.ShapeDtypeStruct((s, d), out_dtype),
        compiler_params=_params(1),
        name="rmsnorm",
    )(x, g.reshape(1, d))


def _attach_ride_along_casts(casts, n_row_steps, n_col_steps, steps_of, in_specs, operands, out_specs, out_shape):
    for src_stack, layer in casts:
        _, r, c = src_stack.shape
        assert r % n_row_steps == 0 and (r // n_row_steps) % 16 == 0
        rb = r // n_row_steps
        cb = -(-pl.cdiv(c, n_col_steps) // 128) * 128
        last = pl.cdiv(c, cb) - 1

        def block(*idx, last=last):
            row_step, col_step = steps_of(*idx)
            return row_step, jnp.minimum(col_step, last)

        in_specs.append(pl.BlockSpec((None, rb, cb), lambda *idx, layer=layer, block=block: (layer,) + block(*idx)))
        operands.append(src_stack)
        out_specs.append(pl.BlockSpec((rb, cb), block))
        out_shape.append(jax.ShapeDtypeStruct((r, c), BF16))


def _split_cast_refs(refs, n_casts):
    refs = list(refs)
    if n_casts == 0:
        return refs, []
    return refs[n_casts:-n_casts], list(zip(refs[:n_casts], refs[-n_casts:]))


def _do_casts(cast_refs):
    for src_ref, dst_ref in cast_refs:
        dst_ref[...] = src_ref[...].astype(dst_ref.dtype)


def _gelu_exact(x):
    return 0.5 * x * (1.0 + lax.erf(x * (2.0 ** -0.5)))


def _normed_mm_kernel(a_ref, ssq_ref, w_ref, *refs, act, n_casts):
    (o_ref,), cast_refs = _split_cast_refs(refs, n_casts)
    _do_casts(cast_refs)
    acc = jnp.dot(a_ref[...], w_ref[...], preferred_element_type=F32)
    acc = acc * _row_scale(ssq_ref[...], a_ref.shape[1])
    if act == "gelu":
        acc = _gelu_exact(acc)
    o_ref[...] = acc.astype(o_ref.dtype)


def normed_matmul(xg, ssq, w, *, act=None, casts=(), tm=1024, tn=1024):
    m, k = xg.shape
    _, n = w.shape
    tm, tn = _tile(m, tm), _tile(n, tn)
    grid = (m // tm, n // tn)
    in_specs = [pl.BlockSpec((tm, k), lambda i, j: (i, 0)),
                pl.BlockSpec((tm, 1), lambda i, j: (i, 0)),
                pl.BlockSpec((k, tn), lambda i, j: (0, j))]
    operands = [xg, ssq, w]
    out_specs = [pl.BlockSpec((tm, tn), lambda i, j: (i, j))]
    out_shape = [jax.ShapeDtypeStruct((m, n), BF16)]
    _attach_ride_along_casts(casts, *grid, lambda i, j: (i, j), in_specs, operands, out_specs, out_shape)
    outs = pl.pallas_call(
        functools.partial(_normed_mm_kernel, act=act, n_casts=len(casts)),
        grid=grid,
        in_specs=in_specs,
        out_specs=out_specs,
        out_shape=out_shape,
        compiler_params=_params(2),
        name="normed_matmul_" + (act or "plain"),
    )(*operands)
    return outs[0], list(outs[1:])


def _normed_swiglu_kernel(a_ref, ssq_ref, *refs, tiles_per_step, n_tail, n_casts):
    w_refs, refs = refs[:2 * tiles_per_step], refs[2 * tiles_per_step:]
    (o_ref,), cast_refs = _split_cast_refs(refs, n_casts)

    def compute(n_tiles_here):
        _do_casts(cast_refs)
        a = a_ref[...]
        r = _row_scale(ssq_ref[...], a_ref.shape[1])
        for t in range(n_tiles_here):
            gate = jnp.dot(a, w_refs[t][...], preferred_element_type=F32) * r
            up = jnp.dot(a, w_refs[tiles_per_step + t][...], preferred_element_type=F32) * r
            o_ref[:, t * FF_TILE:(t + 1) * FF_TILE] = (gate * jax.nn.sigmoid(gate) * up).astype(o_ref.dtype)

    if n_tail == 0:
        compute(tiles_per_step)
        return
    is_last = pl.program_id(1) == pl.num_programs(1) - 1

    @pl.when(jnp.logical_not(is_last))
    def _():
        compute(tiles_per_step)

    @pl.when(is_last)
    def _():
        compute(n_tail)


def normed_swiglu(xg, ssq, w, *, casts=(), tm=1024, tiles_per_step=3):
    m, k = xg.shape
    d_ff = w.shape[1] // 2
    assert d_ff % FF_TILE == 0
    n_tiles = d_ff // FF_TILE
    tm = _tile(m, tm)
    grid = (m // tm, pl.cdiv(n_tiles, tiles_per_step))

    def w_spec(first_tile, t):
        return pl.BlockSpec((k, FF_TILE),
                            lambda i, j: (0, first_tile + jnp.minimum(tiles_per_step * j + t, n_tiles - 1)))

    in_specs = [pl.BlockSpec((tm, k), lambda i, j: (i, 0)),
                pl.BlockSpec((tm, 1), lambda i, j: (i, 0))]
    in_specs += [w_spec(first, t) for first in (0, n_tiles) for t in range(tiles_per_step)]
    operands = [xg, ssq] + [w] * (2 * tiles_per_step)
    out_specs = [pl.BlockSpec((tm, tiles_per_step * FF_TILE), lambda i, j: (i, j))]
    out_shape = [jax.ShapeDtypeStruct((m, d_ff), BF16)]
    _attach_ride_along_casts(casts, *grid, lambda i, j: (i, j), in_specs, operands, out_specs, out_shape)
    outs = pl.pallas_call(
        functools.partial(_normed_swiglu_kernel, tiles_per_step=tiles_per_step, n_tail=n_tiles % tiles_per_step,
                          n_casts=len(casts)),
        grid=grid,
        in_specs=in_specs,
        out_specs=out_specs,
        out_shape=out_shape,
        compiler_params=_params(2),
        name="normed_swiglu",
    )(*operands)
    return outs[0], list(outs[1:])


def _mm_res_kernel(a_ref, w_ref, r_ref, g_ref, *refs, nk, k_last, emit_norm, n_casts):
    j, kk = pl.program_id(1), pl.program_id(2)
    (o_ref, *norm_refs), cast_refs = _split_cast_refs(refs, n_casts)
    assert len(norm_refs) == (2 if emit_norm else 0)

    def partial_product(k_used=None):
        _do_casts(cast_refs)
        return jnp.dot(a_ref[:, :k_used], w_ref[:k_used, :], preferred_element_type=F32)

    def emit_norm(xn):
        xg_ref, ssq_ref = norm_refs
        xg_ref[...] = (xn * g_ref[...]).astype(xg_ref.dtype)
        s = jnp.sum(xn * xn, axis=-1, keepdims=True)

        @pl.when(j == 0)
        def _():
            ssq_ref[...] = s

        @pl.when(j != 0)
        def _():
            ssq_ref[...] += s

    if nk == 1:
        xn = r_ref[...] + partial_product(k_last)
        o_ref[...] = xn
        if norm_refs:
            emit_norm(xn)
        return

    @pl.when(kk == 0)
    def _():
        o_ref[...] = r_ref[...] + partial_product()

    @pl.when(jnp.logical_and(kk != 0, kk != nk - 1))
    def _():
        o_ref[...] += partial_product()

    @pl.when(kk == nk - 1)
    def _():
        xn = o_ref[...] + partial_product(k_last)
        o_ref[...] = xn
        if norm_refs:
            emit_norm(xn)


def matmul_residual(a, w, res, g_next, *, emit_norm=True, casts=(), tm=1024, tn=1024, tk=4096):
    m, k = a.shape
    _, n = w.shape
    tm, tn, tk = _tile(m, tm), _tile(n, tn), min(tk, k)
    nk = pl.cdiv(k, tk)
    k_last = k - (nk - 1) * tk
    assert tk % 128 == 0 and k_last % 128 == 0
    in_specs = [pl.BlockSpec((tm, tk), lambda i, j, kk: (i, kk)),
                pl.BlockSpec((tk, tn), lambda i, j, kk: (kk, j)),
                pl.BlockSpec((tm, tn), lambda i, j, kk: (i, j)),
                pl.BlockSpec((1, tn), lambda i, j, kk: (0, j))]
    operands = [a, w, res, g_next.reshape(1, n)]
    out_specs = [pl.BlockSpec((tm, tn), lambda i, j, kk: (i, j))]
    out_shape = [jax.ShapeDtypeStruct((m, n), F32)]
    if emit_norm:
        out_specs += [pl.BlockSpec((tm, tn), lambda i, j, kk: (i, j)),
                      pl.BlockSpec((tm, 1), lambda i, j, kk: (i, 0))]
        out_shape += [jax.ShapeDtypeStruct((m, n), BF16), jax.ShapeDtypeStruct((m, 1), F32)]
    n_own = len(out_specs)
    _attach_ride_along_casts(casts, m // tm, (n // tn) * nk, lambda i, j, kk: (i, j * nk + kk),
                             in_specs, operands, out_specs, out_shape)
    outs = list(pl.pallas_call(
        functools.partial(_mm_res_kernel, nk=nk, k_last=k_last, emit_norm=emit_norm, n_casts=len(casts)),
        grid=(m // tm, n // tn, nk),
        in_specs=in_specs,
        out_specs=out_specs,
        out_shape=out_shape,
        compiler_params=_params(3),
        name="matmul_residual",
    )(*operands))
    x_new, xg, ssq = outs[:n_own] if emit_norm else (outs[0], None, None)
    return x_new, xg, ssq, outs[n_own:]


def _sgu_kernel(u_ref, v_ref, g_ref, b_ref, ws_ref, bst_ref, o_ref, *, n_chunks, n_groups):
    v = v_ref[...].astype(F32)
    mu = jnp.mean(v, axis=-1, keepdims=True)
    vc = v - mu
    var = jnp.mean(vc * vc, axis=-1, keepdims=True)
    vn = (vc * lax.rsqrt(var + LN_EPS) * g_ref[...] + b_ref[...]).astype(BF16)

    row = lax.broadcasted_iota(jnp.int32, (CHUNK, CHUNK), 0)
    col = lax.broadcasted_iota(jnp.int32, (CHUNK, CHUNK), 1)
    causal = row >= col
    for g in range(n_groups):
        cols = slice(g * GROUP_DIM, (g + 1) * GROUP_DIM)
        w = jnp.where(causal, ws_ref[g], 0.0).astype(BF16)
        bias = bst_ref[:, g:g + 1]
        for c in range(n_chunks):
            rows = slice(c * CHUNK, (c + 1) * CHUNK)
            s = jnp.dot(w, vn[rows, cols], preferred_element_type=F32) + bias
            o_ref[rows, cols] = (u_ref[rows, cols].astype(F32) * s).astype(o_ref.dtype)


def spatial_gating(z, ln_g, ln_b, w_s, b_s, *, tm=256):
    s, w2 = z.shape
    wdt = w2 // 2
    n_groups = wdt // GROUP_DIM
    tm = _tile(s, tm)
    assert tm % CHUNK == 0 and w_s.shape == (n_groups, CHUNK, CHUNK)
    return pl.pallas_call(
        functools.partial(_sgu_kernel, n_chunks=tm // CHUNK, n_groups=n_groups),
        grid=(s // tm,),
        in_specs=[pl.BlockSpec((tm, wdt), lambda i: (i, 0)),
                  pl.BlockSpec((tm, wdt), lambda i: (i, 1)),
                  pl.BlockSpec((1, wdt), lambda i: (0, 0)),
                  pl.BlockSpec((1, wdt), lambda i: (0, 0)),
                  pl.BlockSpec((n_groups, CHUNK, CHUNK), lambda i: (0, 0, 0)),
                  pl.BlockSpec((CHUNK, n_groups), lambda i: (0, 0))],
        out_specs=pl.BlockSpec((tm, wdt), lambda i: (i, 0)),
        out_shape=jax.ShapeDtypeStruct((s, wdt), BF16),
        compiler_params=_params(1),
        name="spatial_gating",
    )(z, z, ln_g.reshape(1, wdt), ln_b.reshape(1, wdt), w_s, b_s.T)


def _attn_kernel(q_ref, k_ref, v_ref, o_ref, acc_ref, run_ref, *, t, n_sub, scale):
    blk = pl.program_id(1)
    row = lax.broadcasted_iota(jnp.int32, (t, t), 0)
    col = lax.broadcasted_iota(jnp.int32, (t, t), 1)
    later = (row > col).astype(BF16)
    strictly_causal = col < row

    def visit(q, kt, run, keep):
        start = pl.multiple_of(kt * t, t)
        k = k_ref[pl.ds(start, t), :]
        v = v_ref[pl.ds(start, t), :]
        z = lax.dot_general(q, k, (((1,), (1,)), ((), ())), preferred_element_type=F32)
        sp = jnp.maximum(z, 0.0) + jnp.log(1.0 + jnp.exp(-jnp.abs(z)))
        if keep is not None:
            sp = jnp.where(keep, sp, 0.0)
        within = jnp.dot(sp.astype(BF16), later, preferred_element_type=F32)
        a = jnp.exp(z - sp - within - run)
        if keep is not None:
            a = jnp.where(keep, a, 0.0)
        pv = jnp.dot(a.astype(BF16), v, preferred_element_type=F32)
        return pv, within[:, :1] + sp[:, :1]

    def q_rows(sub):
        return q_ref[sub * t:(sub + 1) * t, :] * jnp.asarray(scale, BF16)

    for sub in range(n_sub):
        kd = n_sub * blk + sub
        pv_d, tot_d = visit(q_rows(sub), kd, 0.0, strictly_causal)
        has_prev = blk > 0 if sub == 0 else None
        pv_p, tot_p = visit(q_rows(sub), jnp.maximum(kd - 1, 0), tot_d, has_prev)
        acc_ref[sub] = pv_d + pv_p
        run_ref[sub] = tot_d + tot_p

    def more(n):
        needed = False
        for sub in range(n_sub):
            has_tiles_left = n_sub * blk + sub - 2 - n >= 0
            needed = jnp.logical_or(needed, jnp.logical_and(has_tiles_left, jnp.min(run_ref[sub]) < EXP_ZERO_F32))
        return needed

    def step(n):
        for sub in range(n_sub):
            kt = n_sub * blk + sub - 2 - n
            keep = kt >= 0 if sub < n_sub - 1 else None
            pv, tot = visit(q_rows(sub), jnp.maximum(kt, 0), run_ref[sub], keep)
            acc_ref[sub] += pv
            run_ref[sub] += tot
        return n + 1

    lax.while_loop(more, step, 0)
    for sub in range(n_sub):
        o_ref[sub * t:(sub + 1) * t, :] = acc_ref[sub].astype(o_ref.dtype)


def stick_breaking_attention(qkv, *, t=256, n_sub=4):
    s, d3 = qkv.shape
    d = d3 // 3
    n_heads = d // HEAD_DIM
    assert s % (n_sub * t) == 0
    assert HEAD_DIM == 4 ** round(math.log(HEAD_DIM, 4)), "1/sqrt(HEAD_DIM) must be a power of two"
    rows = n_sub * t
    return pl.pallas_call(
        functools.partial(_attn_kernel, t=t, n_sub=n_sub, scale=HEAD_DIM ** -0.5),
        grid=(n_heads, s // rows),
        in_specs=[pl.BlockSpec((rows, HEAD_DIM), lambda h, i: (i, h)),
                  pl.BlockSpec((s, HEAD_DIM), lambda h, i: (0, n_heads + h)),
                  pl.BlockSpec((s, HEAD_DIM), lambda h, i: (0, 2 * n_heads + h))],
        out_specs=pl.BlockSpec((rows, HEAD_DIM), lambda h, i: (i, h)),
        out_shape=jax.ShapeDtypeStruct((s, d), BF16),
        scratch_shapes=[pltpu.VMEM((n_sub, t, HEAD_DIM), F32),
                        pltpu.VMEM((n_sub, t, 1), F32)],
        compiler_params=_params(2),
        name="stick_breaking_attention",
    )(qkv, qkv, qkv)


def kernel(x, norm_mix, a_w_in, a_ln_g, a_ln_b, a_w_s, a_b_s, a_w_out, b_w_qkv, b_w_o,
           norm_ffn, ffn_w_gate_up, ffn_w_down, final_norm):
    b, s, d = x.shape
    assert b == 1, "kernels treat the sequence as one batch row"
    depth = norm_mix.shape[0]
    d_ff = ffn_w_down.shape[1]
    tk_down = -(-d_ff // (DOWN_K_TILES * FF_TILE)) * FF_TILE
    def mixer_stacks(i):
        return (a_w_in, a_w_out) if i % 2 == 0 else (b_w_qkv, b_w_o)

    w_mix_in = mixer_stacks(0)[0][0].astype(BF16)
    w_mix_out = w_gate_up = w_down = None
    xs = x.reshape(s, d)
    xg, ssq = gain_sumsq(xs, norm_mix[0])
    for i in range(depth):
        j = i // 2
        first, last = i == 0, i + 1 == depth
        nxt_in, nxt_out = ([(stack, (i + 1) // 2)] for stack in mixer_stacks(i + 1)) if not last else ([], [])
        own_out = [(mixer_stacks(0)[1], 0)] if first else []
        own_ffn = [(ffn_w_gate_up, 0), (ffn_w_down, 0)] if first else []
        act = "gelu" if i % 2 == 0 else None
        mixed, cast = normed_matmul(xg, ssq, w_mix_in, act=act, casts=nxt_in + own_out)
        nxt_w_mix_in, own_w_mix_out = cast[:len(nxt_in)], cast[len(nxt_in):]
        if first:
            (w_mix_out,) = own_w_mix_out
        if i % 2 == 0:
            y = spatial_gating(mixed, a_ln_g[j], a_ln_b[j], a_w_s[j], a_b_s[j])
        else:
            y = stick_breaking_attention(mixed)
        xs, xg, ssq, cast = matmul_residual(y, w_mix_out, xs, norm_ffn[i], casts=nxt_out + own_ffn, tn=512)
        nxt_w_mix_out, own_w_ffn = cast[:len(nxt_out)], cast[len(nxt_out):]
        if first:
            w_gate_up, w_down = own_w_ffn
        hid, nxt_w_gate_up = normed_swiglu(xg, ssq, w_gate_up, casts=[] if last else [(ffn_w_gate_up, i + 1)])
        xs, xg, ssq, nxt_w_down = matmul_residual(
            hid, w_down, xs, final_norm if last else norm_mix[i + 1], emit_norm=not last,
            casts=[] if last else [(ffn_w_down, i + 1)], tk=tk_down)
        if not last:
            (w_mix_in,), (w_mix_out,), (w_gate_up,), (w_down,) = nxt_w_mix_in, nxt_w_mix_out, nxt_w_gate_up, nxt_w_down
    out = rmsnorm(xs, final_norm, F32)
    return out.reshape(b, s, d)
```
